```python
import math
import jax, jax.numpy as jnp
from jax import lax
import numpy as np

D_MODEL = 1024
BATCH = 8
SEQ = 4096
DEPTH = 4

ATTN_WIDTH = D_MODEL // 2
SSM_WIDTH = D_MODEL - ATTN_WIDTH
HEAD_DIM = 64
N_Q_HEADS = ATTN_WIDTH // HEAD_DIM
N_KV_HEADS = 2
Q_PER_KV = N_Q_HEADS // N_KV_HEADS
KV_WIDTH = N_KV_HEADS * HEAD_DIM
WINDOW = 128
BLOCK = 128
SSM_GROUP = 16
N_SSM_GROUPS = SSM_WIDTH // SSM_GROUP
STATE = 64
D_FF = 4 * D_MODEL
IN_WIDTH = ATTN_WIDTH + 2 * KV_WIDTH + SSM_WIDTH
N_MOD = 6
EPS = 1e-6
NEG_INF = -1e30
DT_MIN = 1e-3
DT_MAX = 1e-1

kernel_name = "hymba_swa_s5_sqrelu_adaln"


def rmsnorm(x, g):
    xf = x.astype(jnp.float32)
    y = xf * lax.rsqrt(jnp.mean(xf * xf, axis=-1, keepdims=True) + EPS)
    return (y * g.astype(jnp.float32)).astype(x.dtype)


def alibi_slopes():
    s = 2.0 ** (-8.0 * np.arange(1, N_Q_HEADS + 1) / N_Q_HEADS)
    return jnp.asarray(s, dtype=jnp.float32).reshape(N_KV_HEADS, Q_PER_KV)


def sliding_window_attention(q, k, v, sinks):
    b, l = q.shape[0], q.shape[1]
    nb = l // BLOCK
    qb = q.reshape(b, nb, BLOCK, N_KV_HEADS, Q_PER_KV, HEAD_DIM)

    def band(t):
        t = t.reshape(b, l, N_KV_HEADS, HEAD_DIM)
        tp = jnp.pad(t, ((0, 0), (BLOCK, 0), (0, 0), (0, 0)))
        tb = tp.reshape(b, nb + 1, BLOCK, N_KV_HEADS, HEAD_DIM)
        return jnp.concatenate([tb[:, :-1], tb[:, 1:]], axis=2)

    kb, vb = band(k), band(v)
    scores = jnp.einsum('bnqhgd,bnshd->bnhgqs', qb, kb).astype(jnp.float32) * (HEAD_DIM ** -0.5)

    r = jnp.arange(BLOCK)[:, None]
    j = jnp.arange(2 * BLOCK)[None, :]
    diff = BLOCK + r - j
    key_pos = (jnp.arange(nb)[:, None, None] - 1) * BLOCK + j[None]
    valid = ((diff >= 0) & (diff < WINDOW))[None] & (key_pos >= 0)
    bias = -alibi_slopes()[:, :, None, None] * diff.astype(jnp.float32)
    scores = jnp.where(valid[None, :, None, None], scores + bias, NEG_INF)

    sink = jnp.broadcast_to(sinks.astype(jnp.float32).reshape(1, 1, N_KV_HEADS, Q_PER_KV, 1, 1),
                            scores.shape[:-1] + (1,))
    probs = jax.nn.softmax(jnp.concatenate([scores, sink], axis=-1), axis=-1)[..., :-1]
    out = jnp.einsum('bnhgqs,bnshd->bnqhgd', probs.astype(v.dtype), vb)
    return out.reshape(b, l, ATTN_WIDTH)


def s5_mixer(u, lam_re, lam_im, log_dt, b_re, b_im, c_re, c_im, d_skip, w_glu, b_glu):
    bsz, l = u.shape[0], u.shape[1]
    uf = u.astype(jnp.float32)
    ug = uf.reshape(bsz, l, N_SSM_GROUPS, SSM_GROUP)
    dt = jnp.exp(log_dt.astype(jnp.float32))[:, None]
    lr = lam_re.astype(jnp.float32)
    li = lam_im.astype(jnp.float32)
    mag = jnp.exp(lr * dt)
    ang = li * dt
    ab_r = mag * jnp.cos(ang)
    ab_i = mag * jnp.sin(ang)
    nr = ab_r - 1.0
    ni = ab_i
    den = lr * lr + li * li
    f_r = (nr * lr + ni * li) / den
    f_i = (ni * lr - nr * li) / den
    br = b_re.astype(jnp.float32)
    bi = b_im.astype(jnp.float32)
    bb_r = f_r[..., None] * br - f_i[..., None] * bi
    bb_i = f_r[..., None] * bi + f_i[..., None] * br
    bu_r = jnp.einsum('blgc,gpc->blgp', ug, bb_r)
    bu_i = jnp.einsum('blgc,gpc->blgp', ug, bb_i)
    a_r = jnp.broadcast_to(ab_r, bu_r.shape)
    a_i = jnp.broadcast_to(ab_i, bu_i.shape)

    def combine(e1, e2):
        a1r, a1i, b1r, b1i = e1
        a2r, a2i, b2r, b2i = e2
        return (a2r * a1r - a2i * a1i,
                a2r * a1i + a2i * a1r,
                a2r * b1r - a2i * b1i + b2r,
                a2r * b1i + a2i * b1r + b2i)

    _, _, h_r, h_i = lax.associative_scan(combine, (a_r, a_i, bu_r, bu_i), axis=1)
    y = (jnp.einsum('blgp,gcp->blgc', h_r, c_re.astype(jnp.float32))
         - jnp.einsum('blgp,gcp->blgc', h_i, c_im.astype(jnp.float32)))
    y = y.reshape(bsz, l, SSM_WIDTH) + d_skip.astype(jnp.float32) * uf
    z = jax.nn.gelu(y).astype(u.dtype)
    return z * jax.nn.sigmoid(z @ w_glu + b_glu)


def setup_inputs(seed: int = 0) -> dict:
    key = jax.random.key(seed)
    ks = jax.random.split(key, 32)
    f32 = jnp.float32

    def nrm(k, shape, scale):
        return jax.random.normal(k, shape, f32) * scale

    def gain(k, shape):
        return 1.0 + 0.05 * jax.random.normal(k, shape, f32)

    L, G, P, C = DEPTH, N_SSM_GROUPS, STATE, SSM_GROUP
    n_idx = jnp.arange(P, dtype=f32)[None, None, :]
    return {
        "x": nrm(ks[0], (BATCH, SEQ, D_MODEL), 1.0),
        "c": nrm(ks[1], (BATCH, D_MODEL), 1.0),
        "w_ada": nrm(ks[2], (L, D_MODEL, N_MOD * D_MODEL), 0.5 * D_MODEL ** -0.5),
        "b_ada": nrm(ks[3], (L, N_MOD * D_MODEL), 0.02),
        "pre_mix_g": gain(ks[4], (L, D_MODEL)),
        "w_in": nrm(ks[5], (L, D_MODEL, IN_WIDTH), D_MODEL ** -0.5),
        "attn_sinks": nrm(ks[6], (L, N_Q_HEADS), 0.5),
        "lam_re": -0.5 * jnp.exp(0.05 * jax.random.normal(ks[7], (L, G, P), f32)),
        "lam_im": math.pi * n_idx + 0.01 * jax.random.normal(ks[8], (L, G, P), f32),
        "log_dt": jax.random.uniform(ks[9], (L, G), f32, math.log(DT_MIN), math.log(DT_MAX)),
        "b_re": nrm(ks[10], (L, G, P, C), (2.0 * C) ** -0.5),
        "b_im": nrm(ks[11], (L, G, P, C), (2.0 * C) ** -0.5),
        "c_re": nrm(ks[12], (L, G, C, P), (2.0 * P) ** -0.5 * 4.0),
        "c_im": nrm(ks[13], (L, G, C, P), (2.0 * P) ** -0.5 * 4.0),
        "d_skip": nrm(ks[14], (L, SSM_WIDTH), 1.0),
        "w_glu": nrm(ks[15], (L, SSM_WIDTH, SSM_WIDTH), SSM_WIDTH ** -0.5),
        "b_glu": nrm(ks[16], (L, SSM_WIDTH), 0.02),
        "attn_out_g": gain(ks[17], (L, ATTN_WIDTH)),
        "ssm_out_g": gain(ks[18], (L, SSM_WIDTH)),
        "w_out": nrm(ks[19], (L, D_MODEL, D_MODEL), D_MODEL ** -0.5),
        "post_mix_g": gain(ks[20], (L, D_MODEL)),
        "pre_mlp_g": gain(ks[21], (L, D_MODEL)),
        "w_mlp_in": nrm(ks[22], (L, D_MODEL, D_FF), D_MODEL ** -0.5),
        "w_mlp_out": nrm(ks[23], (L, D_FF, D_MODEL), D_FF ** -0.5),
        "post_mlp_g": gain(ks[24], (L, D_MODEL)),
    }


def reference(x, c, w_ada, b_ada, pre_mix_g, w_in, attn_sinks, lam_re, lam_im, log_dt,
              b_re, b_im, c_re, c_im, d_skip, w_glu, b_glu, attn_out_g, ssm_out_g, w_out,
              post_mix_g, pre_mlp_g, w_mlp_in, w_mlp_out, post_mlp_g):
    c_act = jax.nn.silu(c)
    split_pts = [ATTN_WIDTH, ATTN_WIDTH + KV_WIDTH, ATTN_WIDTH + 2 * KV_WIDTH]
    for i in range(DEPTH):
        mod = c_act @ w_ada[i] + b_ada[i]
        sh1, sc1, g1, sh2, sc2, g2 = [m[:, None, :] for m in jnp.split(mod, N_MOD, axis=-1)]

        h = rmsnorm(x, pre_mix_g[i]) * (1.0 + sc1) + sh1
        proj = h @ w_in[i]
        q, k, v, u = jnp.split(proj, split_pts, axis=-1)
        attn = sliding_window_attention(q, k, v, attn_sinks[i])
        ssm = s5_mixer(u, lam_re[i], lam_im[i], log_dt[i], b_re[i], b_im[i], c_re[i], c_im[i],
                       d_skip[i], w_glu[i], b_glu[i])
        heads = jnp.concatenate([rmsnorm(attn, attn_out_g[i]), rmsnorm(ssm, ssm_out_g[i])], axis=-1)
        mixed = heads @ w_out[i]
        x = x + g1 * rmsnorm(mixed, post_mix_g[i])

        h = rmsnorm(x, pre_mlp_g[i]) * (1.0 + sc2) + sh2
        f = jnp.square(jax.nn.relu(h @ w_mlp_in[i])) @ w_mlp_out[i]
        x = x + g2 * rmsnorm(f, post_mlp_g[i])
    return x
```

```python
import functools

import numpy as np
import jax
import jax.numpy as jnp
from jax import lax
from jax.experimental import pallas as pl
from jax.experimental.pallas import tpu as pltpu

D_MODEL = 1024
BATCH = 8
SEQ = 4096
DEPTH = 4
ATTN_WIDTH = 512
SSM_WIDTH = 512
HEAD_DIM = 64
N_Q_HEADS = 8
N_KV_HEADS = 2
Q_PER_KV = 4
KV_WIDTH = 128
WINDOW = 128
SSM_GROUP = 16
N_SSM_GROUPS = 32
STATE = 64
D_FF = 4096
IN_WIDTH = 1280
N_MOD = 6
EPS = 1e-6
NEG_INF = -1e30

LANES = 128
SUBLANES = 8
MXU_DIM = 256

STATE_COLS = N_SSM_GROUPS * STATE
BF16 = jnp.bfloat16
F32 = jnp.float32

VMEM_LIMIT = 56 * 1024 * 1024

TM_IN = 1024
TT_SSM = 64
TQ_ATT = 512
TM_MLP = 512
FF_CHUNK = 1024
SCAN_COLS = 1024


def _rms(x, g):
    return x * lax.rsqrt(jnp.mean(x * x, axis=-1, keepdims=True) + EPS) * g


def _mod_kernel(c_ref, w_ref, b_ref, o_ref):
    c = c_ref[...]
    ca = (c * jax.nn.sigmoid(c)).astype(BF16)
    o_ref[0] = jnp.dot(ca, w_ref[0].astype(BF16), preferred_element_type=F32) + b_ref[0]


def _modulation(c, w_ada, b_ada):
    nb = 1536
    return pl.pallas_call(
        _mod_kernel,
        grid=(DEPTH, N_MOD * D_MODEL // nb),
        in_specs=[
            pl.BlockSpec((BATCH, D_MODEL), lambda l, j: (0, 0)),
            pl.BlockSpec((1, D_MODEL, nb), lambda l, j: (l, 0, j)),
            pl.BlockSpec((1, 1, nb), lambda l, j: (l, 0, j)),
        ],
        out_specs=pl.BlockSpec((1, BATCH, nb), lambda l, j: (l, 0, j)),
        out_shape=jax.ShapeDtypeStruct((DEPTH, BATCH, N_MOD * D_MODEL), F32),
        compiler_params=pltpu.CompilerParams(
            dimension_semantics=("arbitrary", "arbitrary"), vmem_limit_bytes=VMEM_LIMIT),
        name="adaln_mod",
    )(c, w_ada, b_ada.reshape(DEPTH, 1, N_MOD * D_MODEL))


def _inproj_kernel(x_ref, mod_ref, g_ref, w_ref, q_ref, kv_ref, u_ref):
    x = x_ref[0]
    mod = mod_ref[0, 0]
    h = _rms(x, g_ref[0]) * (1.0 + mod[1:2]) + mod[0:1]
    proj = jnp.dot(h.astype(BF16), w_ref[0], preferred_element_type=F32)
    q_ref[0] = (proj[:, :ATTN_WIDTH] * (HEAD_DIM ** -0.5)).astype(BF16)
    k = proj[:, ATTN_WIDTH:ATTN_WIDTH + KV_WIDTH]
    v = proj[:, ATTN_WIDTH + KV_WIDTH:ATTN_WIDTH + 2 * KV_WIDTH]
    kv_ref[0, :, 0 * KV_WIDTH:1 * KV_WIDTH] = k.astype(BF16)
    kv_ref[0, :, 1 * KV_WIDTH:2 * KV_WIDTH] = v.astype(BF16)
    kv_ref[0, :, 2 * KV_WIDTH:3 * KV_WIDTH] = pltpu.roll(k, HEAD_DIM, 1).astype(BF16)
    kv_ref[0, :, 3 * KV_WIDTH:4 * KV_WIDTH] = pltpu.roll(v, HEAD_DIM, 1).astype(BF16)
    u_ref[...] = proj[:, ATTN_WIDTH + 2 * KV_WIDTH:]


def _inproj(layer, x, mod, pre_g, w_in):
    nt = SEQ // TM_IN
    return pl.pallas_call(
        _inproj_kernel,
        grid=(BATCH, nt),
        in_specs=[
            pl.BlockSpec((1, TM_IN, D_MODEL), lambda b, i: (b, i, 0)),
            pl.BlockSpec((1, 1, N_MOD, D_MODEL), lambda b, i: (layer, b, 0, 0)),
            pl.BlockSpec((1, 1, D_MODEL), lambda b, i: (layer, 0, 0)),
            pl.BlockSpec((1, D_MODEL, IN_WIDTH), lambda b, i: (layer, 0, 0)),
        ],
        out_specs=[
            pl.BlockSpec((1, TM_IN, ATTN_WIDTH), lambda b, i: (b, i, 0)),
            pl.BlockSpec((1, TM_IN, 4 * KV_WIDTH), lambda b, i: (b, i, 0)),
            pl.BlockSpec((TM_IN, SSM_WIDTH), lambda b, i: (i, b)),
        ],
        out_shape=[
            jax.ShapeDtypeStruct((BATCH, SEQ, ATTN_WIDTH), BF16),
            jax.ShapeDtypeStruct((BATCH, SEQ, 4 * KV_WIDTH), BF16),
            jax.ShapeDtypeStruct((SEQ, BATCH * SSM_WIDTH), F32),
        ],
        compiler_params=pltpu.CompilerParams(
            dimension_semantics=("arbitrary", "arbitrary"), vmem_limit_bytes=VMEM_LIMIT),
        name="in_proj",
    )(x, mod, pre_g, w_in)


def _ssm_kernel(u_ref, wb_ref, are_ref, aim_ref, wcre_ref, wcim_ref, d_ref, wglu_ref, bglu_ref,
                g_ref, o_ref, hs_ref, state_ref):
    @pl.when(pl.program_id(0) == 0)
    def _():
        state_ref[...] = jnp.zeros_like(state_ref)

    u = u_ref[...]
    ub = u.astype(BF16)
    n_tiles = 2 * STATE_COLS // MXU_DIM
    for n in range(n_tiles):
        slab = (n % (n_tiles // 2)) // 2
        hs_ref[:, n * MXU_DIM:(n + 1) * MXU_DIM] = jnp.dot(
            ub[:, slab * LANES:(slab + 1) * LANES], wb_ref[0, n], preferred_element_type=F32)

    for c0 in range(0, STATE_COLS, SCAN_COLS):
        re = slice(c0, c0 + SCAN_COLS)
        im = slice(STATE_COLS + c0, STATE_COLS + c0 + SCAN_COLS)
        ar = jnp.broadcast_to(are_ref[0, :, re], (BATCH, SCAN_COLS))
        ai = jnp.broadcast_to(aim_ref[0, :, re], (BATCH, SCAN_COLS))

        def step(t, carry):
            hr, hi = carry
            rows = pl.ds(pl.multiple_of(t * BATCH, BATCH), BATCH)
            nr = ar * hr - ai * hi + hs_ref[rows, re]
            ni = ar * hi + ai * hr + hs_ref[rows, im]
            hs_ref[rows, re] = nr
            hs_ref[rows, im] = ni
            return nr, ni

        hr, hi = lax.fori_loop(0, TT_SSM, step, (state_ref[:, re], state_ref[:, im]), unroll=4)
        state_ref[:, re] = hr
        state_ref[:, im] = hi

    half = STATE_COLS // 2
    ys = []
    for n in range(2):
        yr = jnp.dot(hs_ref[:, n * half:(n + 1) * half].astype(BF16), wcre_ref[0, n],
                     preferred_element_type=F32)
        yi = jnp.dot(hs_ref[:, STATE_COLS + n * half:STATE_COLS + (n + 1) * half].astype(BF16),
                     wcim_ref[0, n], preferred_element_type=F32)
        ys.append(yr + yi)
    y = jnp.concatenate(ys, axis=1) + d_ref[0] * u
    z = jax.nn.gelu(y)
    gate = jax.nn.sigmoid(jnp.dot(z.astype(BF16), wglu_ref[0], preferred_element_type=F32) + bglu_ref[0])
    o_ref[...] = _rms(z * gate, g_ref[0]).astype(BF16)


def _ssm(layer, u_tm, wb, a_re, a_im, wcre, wcim, d_skip, w_glu, b_glu, ssm_g):
    rows = TT_SSM * BATCH
    vec = lambda n: pl.BlockSpec((1, 1, n), lambda i: (layer, 0, 0))
    return pl.pallas_call(
        _ssm_kernel,
        grid=(SEQ // TT_SSM,),
        in_specs=[
            pl.BlockSpec((rows, SSM_WIDTH), lambda i: (i, 0)),
            pl.BlockSpec((1,) + wb.shape[1:], lambda i: (layer, 0, 0, 0)),
            vec(STATE_COLS), vec(STATE_COLS),
            pl.BlockSpec((1,) + wcre.shape[1:], lambda i: (layer, 0, 0, 0)),
            pl.BlockSpec((1,) + wcim.shape[1:], lambda i: (layer, 0, 0, 0)),
            vec(SSM_WIDTH),
            pl.BlockSpec((1, SSM_WIDTH, SSM_WIDTH), lambda i: (layer, 0, 0)),
            vec(SSM_WIDTH), vec(SSM_WIDTH),
        ],
        out_specs=pl.BlockSpec((rows, SSM_WIDTH), lambda i: (i, 0)),
        out_shape=jax.ShapeDtypeStruct((SEQ * BATCH, SSM_WIDTH), BF16),
        scratch_shapes=[
            pltpu.VMEM((rows, 2 * STATE_COLS), F32),
            pltpu.VMEM((BATCH, 2 * STATE_COLS), F32),
        ],
        compiler_params=pltpu.CompilerParams(
            dimension_semantics=("arbitrary",), vmem_limit_bytes=VMEM_LIMIT),
        name="s5_mixer",
    )(u_tm, wb, a_re, a_im, wcre, wcim, d_skip, w_glu, b_glu, ssm_g)


def _attn_kernel(q_ref, kvc_ref, kvp_ref, ssm_ref, x_ref, mod_ref, sink_ref, ag_ref, wout_ref,
                 pg_ref, o_ref, kpad_ref, vpad_ref, bias_ref, heads_ref):
    i = pl.program_id(1)
    nq = TQ_ATT // WINDOW

    @pl.when((pl.program_id(0) == 0) & (i == 0))
    def _():
        r = lax.broadcasted_iota(jnp.int32, (WINDOW, 2 * WINDOW), 0)
        j = lax.broadcasted_iota(jnp.int32, (WINDOW, 2 * WINDOW), 1)
        diff = WINDOW + r - j
        valid = (diff >= 0) & (diff < WINDOW)
        for h in range(N_Q_HEADS):
            slope = 2.0 ** (-8.0 * (h + 1) / N_Q_HEADS)
            bias_ref[h] = jnp.where(valid, -slope * diff.astype(F32), NEG_INF)

    lane = lax.broadcasted_iota(jnp.int32, (1, LANES), 1)
    for src, rows in ((kvp_ref, slice(0, WINDOW)), (kvc_ref, slice(WINDOW, WINDOW + TQ_ATT))):
        for hk in range(N_KV_HEADS):
            for par in range(2):
                keep = (lane >= par * HEAD_DIM) & (lane < (par + 1) * HEAD_DIM)
                off = 0 if par == hk else 2 * KV_WIDTH
                kpad_ref[hk * 2 + par, rows, :] = jnp.where(
                    keep, src[0, :, off:off + KV_WIDTH], jnp.zeros((), BF16))
                vpad_ref[hk * 2 + par, rows, :] = jnp.where(
                    keep, src[0, :, off + KV_WIDTH:off + 2 * KV_WIDTH], jnp.zeros((), BF16))

    col = lax.broadcasted_iota(jnp.int32, (WINDOW, 2 * WINDOW), 1)

    def qblock(jb, carry):
        q0 = pl.multiple_of(jb * WINDOW, WINDOW)
        first = (i == 0) & (jb == 0)
        for lt in range(N_Q_HEADS // 2):
            hk = (2 * lt) // Q_PER_KV
            qt = q_ref[0, pl.ds(q0, WINDOW), lt * LANES:(lt + 1) * LANES]
            acc = None
            for par in range(2):
                h = 2 * lt + par
                kp = kpad_ref[hk * 2 + par, pl.ds(q0, 2 * WINDOW), :]
                vp = vpad_ref[hk * 2 + par, pl.ds(q0, 2 * WINDOW), :]
                s = lax.dot_general(qt, kp, (((1,), (1,)), ((), ())), preferred_element_type=F32)
                s = s + jnp.where(first & (col < WINDOW), NEG_INF, bias_ref[h])
                sink = sink_ref[0, 0, h]
                m = jnp.maximum(jnp.max(s, axis=-1, keepdims=True), sink)
                e = jnp.exp(s - m)
                denom = jnp.sum(e, axis=-1, keepdims=True) + jnp.exp(sink - m)
                o = jnp.dot(e.astype(BF16), vp, preferred_element_type=F32) / denom
                acc = o if acc is None else acc + o
            heads_ref[pl.ds(q0, WINDOW), lt * LANES:(lt + 1) * LANES] = acc
        return carry

    lax.fori_loop(0, nq, qblock, 0)

    attn = _rms(heads_ref[...], ag_ref[0]).astype(BF16)
    mixed = (jnp.dot(attn, wout_ref[0, :ATTN_WIDTH, :], preferred_element_type=F32)
             + jnp.dot(ssm_ref[...], wout_ref[0, ATTN_WIDTH:, :], preferred_element_type=F32))
    mod = mod_ref[0, 0]
    o_ref[0] = x_ref[0] + mod[2:3] * _rms(mixed, pg_ref[0])


def _attn(layer, q, kv, ssm_tm, x, mod, sinks, attn_g, w_out, post_g):
    nt = SEQ // TQ_ATT
    per_tile = TQ_ATT // WINDOW
    vec = lambda n: pl.BlockSpec((1, 1, n), lambda b, i: (layer, 0, 0))
    return pl.pallas_call(
        _attn_kernel,
        grid=(BATCH, nt),
        in_specs=[
            pl.BlockSpec((1, TQ_ATT, ATTN_WIDTH), lambda b, i: (b, i, 0)),
            pl.BlockSpec((1, TQ_ATT, 4 * KV_WIDTH), lambda b, i: (b, i, 0)),
            pl.BlockSpec((1, WINDOW, 4 * KV_WIDTH),
                         lambda b, i: (b, jnp.maximum(i * per_tile - 1, 0), 0)),
            pl.BlockSpec((TQ_ATT, SSM_WIDTH), lambda b, i: (i, b)),
            pl.BlockSpec((1, TQ_ATT, D_MODEL), lambda b, i: (b, i, 0)),
            pl.BlockSpec((1, 1, N_MOD, D_MODEL), lambda b, i: (layer, b, 0, 0)),
            pl.BlockSpec((1, 1, N_Q_HEADS), lambda b, i: (layer, 0, 0), memory_space=pltpu.SMEM),
            vec(ATTN_WIDTH),
            pl.BlockSpec((1, D_MODEL, D_MODEL), lambda b, i: (layer, 0, 0)),
            vec(D_MODEL),
        ],
        out_specs=pl.BlockSpec((1, TQ_ATT, D_MODEL), lambda b, i: (b, i, 0)),
        out_shape=jax.ShapeDtypeStruct((BATCH, SEQ, D_MODEL), F32),
        scratch_shapes=[
            pltpu.VMEM((2 * N_KV_HEADS, WINDOW + TQ_ATT, LANES), BF16),
            pltpu.VMEM((2 * N_KV_HEADS, WINDOW + TQ_ATT, LANES), BF16),
            pltpu.VMEM((N_Q_HEADS, WINDOW, 2 * WINDOW), F32),
            pltpu.VMEM((TQ_ATT, ATTN_WIDTH), F32),
        ],
        compiler_params=pltpu.CompilerParams(
            dimension_semantics=("arbitrary", "arbitrary"), vmem_limit_bytes=VMEM_LIMIT),
        name="attn_outproj",
    )(q, kv, kv, ssm_tm, x, mod, sinks, attn_g, w_out, post_g)


def _mlp_kernel(x_ref, mod_ref, g_ref, w1_ref, w2_ref, pg_ref, o_ref, acc_ref):
    x = x_ref[0]
    mod = mod_ref[0, 0]
    h = (_rms(x, g_ref[0]) * (1.0 + mod[4:5]) + mod[3:4]).astype(BF16)
    for c in range(D_FF // FF_CHUNK):
        cols = slice(c * FF_CHUNK, (c + 1) * FF_CHUNK)
        f = jnp.dot(h, w1_ref[0, :, cols], preferred_element_type=F32)
        f = jnp.square(jnp.maximum(f, 0.0)).astype(BF16)
        part = jnp.dot(f, w2_ref[0, cols, :], preferred_element_type=F32)
        if c == 0:
            acc_ref[...] = part
        else:
            acc_ref[...] += part
    o_ref[0] = x + mod[5:6] * _rms(acc_ref[...], pg_ref[0])


def _mlp(layer, x, mod, pre_g, w1, w2, post_g):
    nt = SEQ // TM_MLP
    vec = lambda n: pl.BlockSpec((1, 1, n), lambda b, i: (layer, 0, 0))
    return pl.pallas_call(
        _mlp_kernel,
        grid=(BATCH, nt),
        in_specs=[
            pl.BlockSpec((1, TM_MLP, D_MODEL), lambda b, i: (b, i, 0)),
            pl.BlockSpec((1, 1, N_MOD, D_MODEL), lambda b, i: (layer, b, 0, 0)),
            vec(D_MODEL),
            pl.BlockSpec((1, D_MODEL, D_FF), lambda b, i: (layer, 0, 0), pipeline_mode=pl.Buffered(1)),
            pl.BlockSpec((1, D_FF, D_MODEL), lambda b, i: (layer, 0, 0), pipeline_mode=pl.Buffered(1)),
            vec(D_MODEL),
        ],
        out_specs=pl.BlockSpec((1, TM_MLP, D_MODEL), lambda b, i: (b, i, 0)),
        out_shape=jax.ShapeDtypeStruct((BATCH, SEQ, D_MODEL), F32),
        scratch_shapes=[pltpu.VMEM((TM_MLP, D_MODEL), F32)],
        compiler_params=pltpu.CompilerParams(
            dimension_semantics=("arbitrary", "arbitrary"), vmem_limit_bytes=VMEM_LIMIT),
        name="mlp",
    )(x, mod, pre_g, w1, w2, post_g)


def _ssm_params(lam_re, lam_im, log_dt, b_re, b_im, c_re, c_im):
    dt = jnp.exp(log_dt)[..., None]
    mag = jnp.exp(lam_re * dt)
    ang = lam_im * dt
    ab_r = mag * jnp.cos(ang)
    ab_i = mag * jnp.sin(ang)
    nr = ab_r - 1.0
    ni = ab_i
    den = lam_re * lam_re + lam_im * lam_im
    f_r = (nr * lam_re + ni * lam_im) / den
    f_i = (ni * lam_re - nr * lam_im) / den
    bb_r = f_r[..., None] * b_re - f_i[..., None] * b_im
    bb_i = f_r[..., None] * b_im + f_i[..., None] * b_re
    eye = jnp.eye(N_SSM_GROUPS, dtype=F32)
    bd_r = jnp.einsum('lgpc,gh->lgchp', bb_r, eye).reshape(DEPTH, SSM_WIDTH, STATE_COLS)
    bd_i = jnp.einsum('lgpc,gh->lgchp', bb_i, eye).reshape(DEPTH, SSM_WIDTH, STATE_COLS)
    tiles = []
    for bd in (bd_r, bd_i):
        for n in range(STATE_COLS // MXU_DIM):
            slab = n // 2
            tiles.append(bd[:, slab * LANES:(slab + 1) * LANES, n * MXU_DIM:(n + 1) * MXU_DIM])
    wb = jnp.stack(tiles, axis=1).astype(BF16)
    cd_r = jnp.einsum('lgcp,gh->lgphc', c_re, eye).reshape(DEPTH, STATE_COLS, SSM_WIDTH)
    cd_i = jnp.einsum('lgcp,gh->lgphc', -c_im, eye).reshape(DEPTH, STATE_COLS, SSM_WIDTH)
    half = STATE_COLS // 2
    cut = lambda cd: jnp.stack(
        [cd[:, n * half:(n + 1) * half, n * MXU_DIM:(n + 1) * MXU_DIM] for n in range(2)],
        axis=1).astype(BF16)
    a_re = ab_r.reshape(DEPTH, 1, STATE_COLS)
    a_im = ab_i.reshape(DEPTH, 1, STATE_COLS)
    return wb, a_re, a_im, cut(cd_r), cut(cd_i)


def kernel(x, c, w_ada, b_ada, pre_mix_g, w_in, attn_sinks, lam_re, lam_im, log_dt, b_re, b_im,
           c_re, c_im, d_skip, w_glu, b_glu, attn_out_g, ssm_out_g, w_out, post_mix_g, pre_mlp_g,
           w_mlp_in, w_mlp_out, post_mlp_g):
    row = lambda a: a.reshape(DEPTH, 1, a.shape[-1])
    mod = _modulation(c, w_ada, b_ada).reshape(DEPTH, BATCH, N_MOD, D_MODEL)
    wb, a_re, a_im, wcre, wcim = _ssm_params(lam_re, lam_im, log_dt, b_re, b_im, c_re, c_im)
    w_in_b = w_in.astype(BF16)
    w_glu_b = w_glu.astype(BF16)
    w_out_b = w_out.astype(BF16)
    w1_b = w_mlp_in.astype(BF16)
    w2_b = w_mlp_out.astype(BF16)
    sinks = row(attn_sinks)
    for layer in range(DEPTH):
        q, kv, u_tm = _inproj(layer, x, mod, row(pre_mix_g), w_in_b)
        ssm_tm = _ssm(layer, u_tm.reshape(SEQ * BATCH, SSM_WIDTH), wb, a_re, a_im, wcre, wcim,
                      row(d_skip), w_glu_b, row(b_glu), row(ssm_out_g))
        x = _attn(layer, q, kv, ssm_tm.reshape(SEQ, BATCH * SSM_WIDTH), x, mod, sinks,
                  row(attn_out_g), w_out_b, row(post_mix_g))
        x = _mlp(layer, x, mod, row(pre_mlp_g), w1_b, w2_b, row(post_mlp_g))
    return x
```

```python
import functools

import numpy as np
import jax
import jax.numpy as jnp
from jax import lax
from jax.experimental import pallas as pl
from jax.experimental.pallas import tpu as pltpu

D_MODEL = 1024
BATCH = 8
SEQ = 4096
DEPTH = 4
ATTN_WIDTH = 512
SSM_WIDTH = 512
HEAD_DIM = 64
N_Q_HEADS = 8
N_KV_HEADS = 2
Q_PER_KV = 4
KV_WIDTH = 128
WINDOW = 128
SSM_GROUP = 16
N_SSM_GROUPS = 32
STATE = 64
D_FF = 4096
IN_WIDTH = 1280
N_MOD = 6
EPS = 1e-6
NEG_INF = -1e30

LANES = 128
SUBLANES = 8
MXU_DIM = 256

STATE_COLS = N_SSM_GROUPS * STATE
BF16 = jnp.bfloat16
F32 = jnp.float32

VMEM_LIMIT = 56 * 1024 * 1024

TM_IN = 1024
TT_SSM = 64
TQ_ATT = 512
TM_MLP = 512
FF_CHUNK = 1024
SCAN_COLS = 1024


def _rms(x, g):
    return x * lax.rsqrt(jnp.mean(x * x, axis=-1, keepdims=True) + EPS) * g


def _mod_kernel(c_ref, w_ref, b_ref, o_ref):
    c = c_ref[...]
    ca = (c * jax.nn.sigmoid(c)).astype(BF16)
    o_ref[0] = jnp.dot(ca, w_ref[0].astype(BF16), preferred_element_type=F32) + b_ref[0]


def _modulation(c, w_ada, b_ada):
    nb = 1536
    return pl.pallas_call(
        _mod_kernel,
        grid=(DEPTH, N_MOD * D_MODEL // nb),
        in_specs=[
            pl.BlockSpec((BATCH, D_MODEL), lambda l, j: (0, 0)),
            pl.BlockSpec((1, D_MODEL, nb), lambda l, j: (l, 0, j)),
            pl.BlockSpec((1, 1, nb), lambda l, j: (l, 0, j)),
        ],
        out_specs=pl.BlockSpec((1, BATCH, nb), lambda l, j: (l, 0, j)),
        out_shape=jax.ShapeDtypeStruct((DEPTH, BATCH, N_MOD * D_MODEL), F32),
        compiler_params=pltpu.CompilerParams(
            dimension_semantics=("arbitrary", "arbitrary"), vmem_limit_bytes=VMEM_LIMIT),
        name="adaln_mod",
    )(c, w_ada, b_ada.reshape(DEPTH, 1, N_MOD * D_MODEL))


def _inproj_kernel(x_ref, mod_ref, g_ref, w_ref, q_ref, kv_ref, u_ref):
    x = x_ref[0]
    mod = mod_ref[0, 0]
    h = _rms(x, g_ref[0]) * (1.0 + mod[1:2]) + mod[0:1]
    proj = jnp.dot(h.astype(BF16), w_ref[0], preferred_element_type=F32)
    q_ref[0] = (proj[:, :ATTN_WIDTH] * (HEAD_DIM ** -0.5)).astype(BF16)
    k = proj[:, ATTN_WIDTH:ATTN_WIDTH + KV_WIDTH]
    v = proj[:, ATTN_WIDTH + KV_WIDTH:ATTN_WIDTH + 2 * KV_WIDTH]
    kv_ref[0, :, 0 * KV_WIDTH:1 * KV_WIDTH] = k.astype(BF16)
    kv_ref[0, :, 1 * KV_WIDTH:2 * KV_WIDTH] = v.astype(BF16)
    kv_ref[0, :, 2 * KV_WIDTH:3 * KV_WIDTH] = pltpu.roll(k, HEAD_DIM, 1).astype(BF16)
    kv_ref[0, :, 3 * KV_WIDTH:4 * KV_WIDTH] = pltpu.roll(v, HEAD_DIM, 1).astype(BF16)
    u_ref[...] = proj[:, ATTN_WIDTH + 2 * KV_WIDTH:]


def _inproj(layer, x, mod, pre_g, w_in):
    nt = SEQ // TM_IN
    return pl.pallas_call(
        _inproj_kernel,
        grid=(BATCH, nt),
        in_specs=[
            pl.BlockSpec((1, TM_IN, D_MODEL), lambda b, i: (b, i, 0)),
            pl.BlockSpec((1, 1, N_MOD, D_MODEL), lambda b, i: (layer, b, 0, 0)),
            pl.BlockSpec((1, 1, D_MODEL), lambda b, i: (layer, 0, 0)),
            pl.BlockSpec((1, D_MODEL, IN_WIDTH), lambda b, i: (layer, 0, 0)),
        ],
        out_specs=[
            pl.BlockSpec((1, TM_IN, ATTN_WIDTH), lambda b, i: (b, i, 0)),
            pl.BlockSpec((1, TM_IN, 4 * KV_WIDTH), lambda b, i: (b, i, 0)),
            pl.BlockSpec((TM_IN, SSM_WIDTH), lambda b, i: (i, b)),
        ],
        out_shape=[
            jax.ShapeDtypeStruct((BATCH, SEQ, ATTN_WIDTH), BF16),
            jax.ShapeDtypeStruct((BATCH, SEQ, 4 * KV_WIDTH), BF16),
            jax.ShapeDtypeStruct((SEQ, BATCH * SSM_WIDTH), F32),
        ],
        compiler_params=pltpu.CompilerParams(
            dimension_semantics=("arbitrary", "arbitrary"), vmem_limit_bytes=VMEM_LIMIT),
        name="in_proj",
    )(x, mod, pre_g, w_in)


def _ssm_kernel(u_ref, wb_ref, are_ref, aim_ref, wcre_ref, wcim_ref, d_ref, wglu_ref, bglu_ref,
                g_ref, o_ref, utb_ref, otb_ref, hs_ref, state_ref):
    @pl.when(pl.program_id(0) == 0)
    def _():
        state_ref[...] = jnp.zeros_like(state_ref)

    n_slab = SSM_WIDTH // LANES
    for b in range(BATCH):
        for s in range(n_slab):
            c0 = b * SSM_WIDTH + s * LANES
            utb_ref[s, pl.ds(b, TT_SSM, stride=BATCH), :] = u_ref[:, c0:c0 + LANES]

    n_tiles = 2 * STATE_COLS // MXU_DIM
    for n in range(n_tiles):
        slab = (n % (n_tiles // 2)) // 2
        hs_ref[:, n * MXU_DIM:(n + 1) * MXU_DIM] = jnp.dot(
            utb_ref[slab].astype(BF16), wb_ref[0, n], preferred_element_type=F32)

    for c0 in range(0, STATE_COLS, SCAN_COLS):
        re = slice(c0, c0 + SCAN_COLS)
        im = slice(STATE_COLS + c0, STATE_COLS + c0 + SCAN_COLS)
        ar = jnp.broadcast_to(are_ref[0, :, re], (BATCH, SCAN_COLS))
        ai = jnp.broadcast_to(aim_ref[0, :, re], (BATCH, SCAN_COLS))

        def step(t, carry):
            hr, hi = carry
            rows = pl.ds(pl.multiple_of(t * BATCH, BATCH), BATCH)
            nr = ar * hr - ai * hi + hs_ref[rows, re]
            ni = ar * hi + ai * hr + hs_ref[rows, im]
            hs_ref[rows, re] = nr
            hs_ref[rows, im] = ni
            return nr, ni

        hr, hi = lax.fori_loop(0, TT_SSM, step, (state_ref[:, re], state_ref[:, im]), unroll=4)
        state_ref[:, re] = hr
        state_ref[:, im] = hi

    half = STATE_COLS // 2
    ys = []
    for n in range(2):
        yr = jnp.dot(hs_ref[:, n * half:(n + 1) * half].astype(BF16), wcre_ref[0, n],
                     preferred_element_type=F32)
        yi = jnp.dot(hs_ref[:, STATE_COLS + n * half:STATE_COLS + (n + 1) * half].astype(BF16),
                     wcim_ref[0, n], preferred_element_type=F32)
        ys.append(yr + yi)
    u = jnp.concatenate([utb_ref[s] for s in range(n_slab)], axis=1)
    y = jnp.concatenate(ys, axis=1) + d_ref[0] * u
    z = jax.nn.gelu(y)
    gate = jax.nn.sigmoid(jnp.dot(z.astype(BF16), wglu_ref[0], preferred_element_type=F32) + bglu_ref[0])
    res = _rms(z * gate, g_ref[0])
    for s in range(n_slab):
        otb_ref[s] = res[:, s * LANES:(s + 1) * LANES]
    for b in range(BATCH):
        for s in range(n_slab):
            c0 = b * SSM_WIDTH + s * LANES
            o_ref[:, c0:c0 + LANES] = otb_ref[s, pl.ds(b, TT_SSM, stride=BATCH), :].astype(BF16)


def _ssm(layer, u_tm, wb, a_re, a_im, wcre, wcim, d_skip, w_glu, b_glu, ssm_g):
    rows = TT_SSM * BATCH
    vec = lambda n: pl.BlockSpec((1, 1, n), lambda i: (layer, 0, 0))
    return pl.pallas_call(
        _ssm_kernel,
        grid=(SEQ // TT_SSM,),
        in_specs=[
            pl.BlockSpec((TT_SSM, BATCH * SSM_WIDTH), lambda i: (i, 0)),
            pl.BlockSpec((1,) + wb.shape[1:], lambda i: (layer, 0, 0, 0)),
            vec(STATE_COLS), vec(STATE_COLS),
            pl.BlockSpec((1,) + wcre.shape[1:], lambda i: (layer, 0, 0, 0)),
            pl.BlockSpec((1,) + wcim.shape[1:], lambda i: (layer, 0, 0, 0)),
            vec(SSM_WIDTH),
            pl.BlockSpec((1, SSM_WIDTH, SSM_WIDTH), lambda i: (layer, 0, 0)),
            vec(SSM_WIDTH), vec(SSM_WIDTH),
        ],
        out_specs=pl.BlockSpec((TT_SSM, BATCH * SSM_WIDTH), lambda i: (i, 0)),
        out_shape=jax.ShapeDtypeStruct((SEQ, BATCH * SSM_WIDTH), BF16),
        scratch_shapes=[
            pltpu.VMEM((SSM_WIDTH // LANES, rows, LANES), F32),
            pltpu.VMEM((SSM_WIDTH // LANES, rows, LANES), F32),
            pltpu.VMEM((rows, 2 * STATE_COLS), F32),
            pltpu.VMEM((BATCH, 2 * STATE_COLS), F32),
        ],
        compiler_params=pltpu.CompilerParams(
            dimension_semantics=("arbitrary",), vmem_limit_bytes=VMEM_LIMIT),
        name="s5_mixer",
    )(u_tm, wb, a_re, a_im, wcre, wcim, d_skip, w_glu, b_glu, ssm_g)


def _attn_kernel(q_ref, kvc_ref, kvp_ref, ssm_ref, x_ref, mod_ref, sink_ref, ag_ref, wout_ref,
                 pg_ref, o_ref, kpad_ref, vpad_ref, bias_ref, heads_ref):
    i = pl.program_id(1)
    nq = TQ_ATT // WINDOW

    @pl.when((pl.program_id(0) == 0) & (i == 0))
    def _():
        r = lax.broadcasted_iota(jnp.int32, (WINDOW, 2 * WINDOW), 0)
        j = lax.broadcasted_iota(jnp.int32, (WINDOW, 2 * WINDOW), 1)
        diff = WINDOW + r - j
        valid = (diff >= 0) & (diff < WINDOW)
        for h in range(N_Q_HEADS):
            slope = 2.0 ** (-8.0 * (h + 1) / N_Q_HEADS)
            bias_ref[h] = jnp.where(valid, -slope * diff.astype(F32), NEG_INF)

    lane = lax.broadcasted_iota(jnp.int32, (1, LANES), 1)
    for src, rows in ((kvp_ref, slice(0, WINDOW)), (kvc_ref, slice(WINDOW, WINDOW + TQ_ATT))):
        for hk in range(N_KV_HEADS):
            for par in range(2):
                keep = (lane >= par * HEAD_DIM) & (lane < (par + 1) * HEAD_DIM)
                off = 0 if par == hk else 2 * KV_WIDTH
                kpad_ref[hk * 2 + par, rows, :] = jnp.where(
                    keep, src[0, :, off:off + KV_WIDTH], jnp.zeros((), BF16))
                vpad_ref[hk * 2 + par, rows, :] = jnp.where(
                    keep, src[0, :, off + KV_WIDTH:off + 2 * KV_WIDTH], jnp.zeros((), BF16))

    col = lax.broadcasted_iota(jnp.int32, (WINDOW, 2 * WINDOW), 1)

    def qblock(jb, carry):
        q0 = pl.multiple_of(jb * WINDOW, WINDOW)
        first = (i == 0) & (jb == 0)
        for lt in range(N_Q_HEADS // 2):
            hk = (2 * lt) // Q_PER_KV
            qt = q_ref[0, pl.ds(q0, WINDOW), lt * LANES:(lt + 1) * LANES]
            acc = None
            for par in range(2):
                h = 2 * lt + par
                kp = kpad_ref[hk * 2 + par, pl.ds(q0, 2 * WINDOW), :]
                vp = vpad_ref[hk * 2 + par, pl.ds(q0, 2 * WINDOW), :]
                s = lax.dot_general(qt, kp, (((1,), (1,)), ((), ())), preferred_element_type=F32)
                s = s + jnp.where(first & (col < WINDOW), NEG_INF, bias_ref[h])
                sink = sink_ref[0, 0, h]
                m = jnp.maximum(jnp.max(s, axis=-1, keepdims=True), sink)
                e = jnp.exp(s - m)
                denom = jnp.sum(e, axis=-1, keepdims=True) + jnp.exp(sink - m)
                o = jnp.dot(e.astype(BF16), vp, preferred_element_type=F32) / denom
                acc = o if acc is None else acc + o
            heads_ref[pl.ds(q0, WINDOW), lt * LANES:(lt + 1) * LANES] = acc
        return carry

    lax.fori_loop(0, nq, qblock, 0)

    attn = _rms(heads_ref[...], ag_ref[0]).astype(BF16)
    mixed = (jnp.dot(attn, wout_ref[0, :ATTN_WIDTH, :], preferred_element_type=F32)
             + jnp.dot(ssm_ref[...], wout_ref[0, ATTN_WIDTH:, :], preferred_element_type=F32))
    mod = mod_ref[0, 0]
    o_ref[0] = x_ref[0] + mod[2:3] * _rms(mixed, pg_ref[0])


def _attn(layer, q, kv, ssm_tm, x, mod, sinks, attn_g, w_out, post_g):
    nt = SEQ // TQ_ATT
    per_tile = TQ_ATT // WINDOW
    vec = lambda n: pl.BlockSpec((1, 1, n), lambda b, i: (layer, 0, 0))
    return pl.pallas_call(
        _attn_kernel,
        grid=(BATCH, nt),
        in_specs=[
            pl.BlockSpec((1, TQ_ATT, ATTN_WIDTH), lambda b, i: (b, i, 0)),
            pl.BlockSpec((1, TQ_ATT, 4 * KV_WIDTH), lambda b, i: (b, i, 0)),
            pl.BlockSpec((1, WINDOW, 4 * KV_WIDTH),
                         lambda b, i: (b, jnp.maximum(i * per_tile - 1, 0), 0)),
            pl.BlockSpec((TQ_ATT, SSM_WIDTH), lambda b, i: (i, b)),
            pl.BlockSpec((1, TQ_ATT, D_MODEL), lambda b, i: (b, i, 0)),
            pl.BlockSpec((1, 1, N_MOD, D_MODEL), lambda b, i: (layer, b, 0, 0)),
            pl.BlockSpec((1, 1, N_Q_HEADS), lambda b, i: (layer, 0, 0), memory_space=pltpu.SMEM),
            vec(ATTN_WIDTH),
            pl.BlockSpec((1, D_MODEL, D_MODEL), lambda b, i: (layer, 0, 0)),
            vec(D_MODEL),
        ],
        out_specs=pl.BlockSpec((1, TQ_ATT, D_MODEL), lambda b, i: (b, i, 0)),
        out_shape=jax.ShapeDtypeStruct((BATCH, SEQ, D_MODEL), F32),
        scratch_shapes=[
            pltpu.VMEM((2 * N_KV_HEADS, WINDOW + TQ_ATT, LANES), BF16),
            pltpu.VMEM((2 * N_KV_HEADS, WINDOW + TQ_ATT, LANES), BF16),
            pltpu.VMEM((N_Q_HEADS, WINDOW, 2 * WINDOW), F32),
            pltpu.VMEM((TQ_ATT, ATTN_WIDTH), F32),
        ],
        compiler_params=pltpu.CompilerParams(
            dimension_semantics=("arbitrary", "arbitrary"), vmem_limit_bytes=VMEM_LIMIT),
        name="attn_outproj",
    )(q, kv, kv, ssm_tm, x, mod, sinks, attn_g, w_out, post_g)


def _mlp_kernel(x_ref, mod_ref, g_ref, w1_ref, w2_ref, pg_ref, o_ref, acc_ref):
    x = x_ref[0]
    mod = mod_ref[0, 0]
    h = (_rms(x, g_ref[0]) * (1.0 + mod[4:5]) + mod[3:4]).astype(BF16)
    for c in range(D_FF // FF_CHUNK):
        cols = slice(c * FF_CHUNK, (c + 1) * FF_CHUNK)
        f = jnp.dot(h, w1_ref[0, :, cols], preferred_element_type=F32)
        f = jnp.square(jnp.maximum(f, 0.0)).astype(BF16)
        part = jnp.dot(f, w2_ref[0, cols, :], preferred_element_type=F32)
        if c == 0:
            acc_ref[...] = part
        else:
            acc_ref[...] += part
    o_ref[0] = x + mod[5:6] * _rms(acc_ref[...], pg_ref[0])


def _mlp(layer, x, mod, pre_g, w1, w2, post_g):
    nt = SEQ // TM_MLP
    vec = lambda n: pl.BlockSpec((1, 1, n), lambda b, i: (layer, 0, 0))
    return pl.pallas_call(
        _mlp_kernel,
        grid=(BATCH, nt),
        in_specs=[
            pl.BlockSpec((1, TM_MLP, D_MODEL), lambda b, i: (b, i, 0)),
            pl.BlockSpec((1, 1, N_MOD, D_MODEL), lambda b, i: (layer, b, 0, 0)),
            vec(D_MODEL),
            pl.BlockSpec((1, D_MODEL, D_FF), lambda b, i: (layer, 0, 0), pipeline_mode=pl.Buffered(1)),
            pl.BlockSpec((1, D_FF, D_MODEL), lambda b, i: (layer, 0, 0), pipeline_mode=pl.Buffered(1)),
            vec(D_MODEL),
        ],
        out_specs=pl.BlockSpec((1, TM_MLP, D_MODEL), lambda b, i: (b, i, 0)),
        out_shape=jax.ShapeDtypeStruct((BATCH, SEQ, D_MODEL), F32),
        scratch_shapes=[pltpu.VMEM((TM_MLP, D_MODEL), F32)],
        compiler_params=pltpu.CompilerParams(
            dimension_semantics=("arbitrary", "arbitrary"), vmem_limit_bytes=VMEM_LIMIT),
        name="mlp",
    )(x, mod, pre_g, w1, w2, post_g)


def _ssm_params(lam_re, lam_im, log_dt, b_re, b_im, c_re, c_im):
    dt = jnp.exp(log_dt)[..., None]
    mag = jnp.exp(lam_re * dt)
    ang = lam_im * dt
    ab_r = mag * jnp.cos(ang)
    ab_i = mag * jnp.sin(ang)
    nr = ab_r - 1.0
    ni = ab_i
    den = lam_re * lam_re + lam_im * lam_im
    f_r = (nr * lam_re + ni * lam_im) / den
    f_i = (ni * lam_re - nr * lam_im) / den
    bb_r = f_r[..., None] * b_re - f_i[..., None] * b_im
    bb_i = f_r[..., None] * b_im + f_i[..., None] * b_re
    gpt = MXU_DIM // STATE
    gps = LANES // SSM_GROUP
    n_t = STATE_COLS // MXU_DIM
    n_i, g_i, h_i = np.meshgrid(np.arange(n_t), np.arange(gps), np.arange(gpt), indexing='ij')
    sel_b = jnp.asarray(g_i == gpt * (n_i % (gps // gpt)) + h_i, F32)

    def b_tiles(bb):
        bb = bb.reshape(DEPTH, n_t, gpt, STATE, SSM_GROUP)
        w = jnp.einsum('lnhpc,ngh->lngchp', bb, sel_b)
        return w.reshape(DEPTH, n_t, LANES, MXU_DIM)

    wb = jnp.concatenate([b_tiles(bb_r), b_tiles(bb_i)], axis=1).astype(BF16)

    gpc = MXU_DIM // SSM_GROUP
    eye = jnp.eye(gpc, dtype=F32)

    def c_tiles(cc):
        cc = cc.reshape(DEPTH, N_SSM_GROUPS // gpc, gpc, SSM_GROUP, STATE)
        w = jnp.einsum('lngcp,gh->lngphc', cc, eye)
        return w.reshape(DEPTH, N_SSM_GROUPS // gpc, gpc * STATE, MXU_DIM).astype(BF16)

    a_re = ab_r.reshape(DEPTH, 1, STATE_COLS)
    a_im = ab_i.reshape(DEPTH, 1, STATE_COLS)
    return wb, a_re, a_im, c_tiles(c_re), c_tiles(-c_im)


def kernel(x, c, w_ada, b_ada, pre_mix_g, w_in, attn_sinks, lam_re, lam_im, log_dt, b_re, b_im,
           c_re, c_im, d_skip, w_glu, b_glu, attn_out_g, ssm_out_g, w_out, post_mix_g, pre_mlp_g,
           w_mlp_in, w_mlp_out, post_mlp_g):
    row = lambda a: a.reshape(DEPTH, 1, a.shape[-1])
    mod = _modulation(c, w_ada, b_ada).reshape(DEPTH, BATCH, N_MOD, D_MODEL)
    wb, a_re, a_im, wcre, wcim = _ssm_params(lam_re, lam_im, log_dt, b_re, b_im, c_re, c_im)
    w_in_b = w_in.astype(BF16)
    w_glu_b = w_glu.astype(BF16)
    w_out_b = w_out.astype(BF16)
    w1_b = w_mlp_in.astype(BF16)
    w2_b = w_mlp_out.astype(BF16)
    sinks = row(attn_sinks)
    for layer in range(DEPTH):
        q, kv, u_tm = _inproj(layer, x, mod, row(pre_mix_g), w_in_b)
        ssm_tm = _ssm(layer, u_tm, wb, a_re, a_im, wcre, wcim,
                      row(d_skip), w_glu_b, row(b_glu), row(ssm_out_g))
        x = _attn(layer, q, kv, ssm_tm, x, mod, sinks,
                  row(attn_out_g), w_out_b, row(post_mix_g))
        x = _mlp(layer, x, mod, row(pre_mlp_g), w1_b, w2_b, row(post_mlp_g))
    return x
```

```python
import functools

import numpy as np
import jax
import jax.numpy as jnp
from jax import lax
from jax.experimental import pallas as pl
from jax.experimental.pallas import tpu as pltpu

D_MODEL = 1024
BATCH = 8
SEQ = 4096
DEPTH = 4
ATTN_WIDTH = 512
SSM_WIDTH = 512
HEAD_DIM = 64
N_Q_HEADS = 8
N_KV_HEADS = 2
Q_PER_KV = 4
KV_WIDTH = 128
WINDOW = 128
SSM_GROUP = 16
N_SSM_GROUPS = 32
STATE = 64
D_FF = 4096
IN_WIDTH = 1280
N_MOD = 6
EPS = 1e-6
NEG_INF = -1e30

LANES = 128
SUBLANES = 8
MXU_DIM = 256

STATE_COLS = N_SSM_GROUPS * STATE
BF16 = jnp.bfloat16
F32 = jnp.float32

VMEM_LIMIT = 56 * 1024 * 1024

TM_IN = 1024
TT_SSM = 64
TQ_ATT = 512
TM_MLP = 512
FF_CHUNK = 1024
SCAN_COLS = LANES * STATE // SSM_GROUP
SCAN_CHUNKS = STATE_COLS // SCAN_COLS


def _rms(x, g):
    return x * lax.rsqrt(jnp.mean(x * x, axis=-1, keepdims=True) + EPS) * g


def _mod_kernel(c_ref, w_ref, b_ref, o_ref):
    c = c_ref[...]
    ca = (c * jax.nn.sigmoid(c)).astype(BF16)
    o_ref[0] = jnp.dot(ca, w_ref[0].astype(BF16), preferred_element_type=F32) + b_ref[0]


def _modulation(c, w_ada, b_ada):
    nb = 1536
    return pl.pallas_call(
        _mod_kernel,
        grid=(DEPTH, N_MOD * D_MODEL // nb),
        in_specs=[
            pl.BlockSpec((BATCH, D_MODEL), lambda l, j: (0, 0)),
            pl.BlockSpec((1, D_MODEL, nb), lambda l, j: (l, 0, j)),
            pl.BlockSpec((1, 1, nb), lambda l, j: (l, 0, j)),
        ],
        out_specs=pl.BlockSpec((1, BATCH, nb), lambda l, j: (l, 0, j)),
        out_shape=jax.ShapeDtypeStruct((DEPTH, BATCH, N_MOD * D_MODEL), F32),
        compiler_params=pltpu.CompilerParams(
            dimension_semantics=("arbitrary", "arbitrary"), vmem_limit_bytes=VMEM_LIMIT),
        name="adaln_mod",
    )(c, w_ada, b_ada.reshape(DEPTH, 1, N_MOD * D_MODEL))


def _inproj_kernel(x_ref, mod_ref, g_ref, w_ref, q_ref, kv_ref, u_ref):
    x = x_ref[0]
    mod = mod_ref[0, 0]
    h = _rms(x, g_ref[0]) * (1.0 + mod[1:2]) + mod[0:1]
    proj = jnp.dot(h.astype(BF16), w_ref[0], preferred_element_type=F32)
    q_ref[0] = (proj[:, :ATTN_WIDTH] * (HEAD_DIM ** -0.5)).astype(BF16)
    k = proj[:, ATTN_WIDTH:ATTN_WIDTH + KV_WIDTH]
    v = proj[:, ATTN_WIDTH + KV_WIDTH:ATTN_WIDTH + 2 * KV_WIDTH]
    kv_ref[0, :, 0 * KV_WIDTH:1 * KV_WIDTH] = k.astype(BF16)
    kv_ref[0, :, 1 * KV_WIDTH:2 * KV_WIDTH] = v.astype(BF16)
    kv_ref[0, :, 2 * KV_WIDTH:3 * KV_WIDTH] = pltpu.roll(k, HEAD_DIM, 1).astype(BF16)
    kv_ref[0, :, 3 * KV_WIDTH:4 * KV_WIDTH] = pltpu.roll(v, HEAD_DIM, 1).astype(BF16)
    u_ref[...] = proj[:, ATTN_WIDTH + 2 * KV_WIDTH:]


def _inproj(layer, x, mod, pre_g, w_in):
    nt = SEQ // TM_IN
    return pl.pallas_call(
        _inproj_kernel,
        grid=(BATCH, nt),
        in_specs=[
            pl.BlockSpec((1, TM_IN, D_MODEL), lambda b, i: (b, i, 0)),
            pl.BlockSpec((1, 1, N_MOD, D_MODEL), lambda b, i: (layer, b, 0, 0)),
            pl.BlockSpec((1, 1, D_MODEL), lambda b, i: (layer, 0, 0)),
            pl.BlockSpec((1, D_MODEL, IN_WIDTH), lambda b, i: (layer, 0, 0)),
        ],
        out_specs=[
            pl.BlockSpec((1, TM_IN, ATTN_WIDTH), lambda b, i: (b, i, 0)),
            pl.BlockSpec((1, TM_IN, 4 * KV_WIDTH), lambda b, i: (b, i, 0)),
            pl.BlockSpec((TM_IN, SSM_WIDTH), lambda b, i: (i, b)),
        ],
        out_shape=[
            jax.ShapeDtypeStruct((BATCH, SEQ, ATTN_WIDTH), BF16),
            jax.ShapeDtypeStruct((BATCH, SEQ, 4 * KV_WIDTH), BF16),
            jax.ShapeDtypeStruct((SEQ, BATCH * SSM_WIDTH), F32),
        ],
        compiler_params=pltpu.CompilerParams(
            dimension_semantics=("arbitrary", "arbitrary"), vmem_limit_bytes=VMEM_LIMIT),
        name="in_proj",
    )(x, mod, pre_g, w_in)


def _ssm_kernel(u_ref, wb_ref, are_ref, aim_ref, wc_ref, d_ref, wglu_ref, bglu_ref, g_ref, o_ref,
                utb_ref, otb_ref, state_ref, *hs_refs):
    @pl.when(pl.program_id(0) == 0)
    def _():
        state_ref[...] = jnp.zeros_like(state_ref)

    n_slab = SSM_WIDTH // LANES
    for b in range(BATCH):
        for s in range(n_slab):
            c0 = b * SSM_WIDTH + s * LANES
            utb_ref[s, pl.ds(b, TT_SSM, stride=BATCH), :] = u_ref[:, c0:c0 + LANES]

    tiles_per_half = SCAN_COLS // MXU_DIM

    def project_in(k):
        ub = utb_ref[k].astype(BF16)
        for part in range(2):
            for j in range(tiles_per_half):
                n = part * (STATE_COLS // MXU_DIM) + k * tiles_per_half + j
                c0 = part * SCAN_COLS + j * MXU_DIM
                hs_refs[k][:, c0:c0 + MXU_DIM] = jnp.dot(ub, wb_ref[0, n], preferred_element_type=F32)

    def scan(k):
        hs = hs_refs[k]
        cols = slice(k * SCAN_COLS, (k + 1) * SCAN_COLS)
        ar = jnp.broadcast_to(are_ref[0, :, cols], (BATCH, SCAN_COLS))
        ai = jnp.broadcast_to(aim_ref[0, :, cols], (BATCH, SCAN_COLS))
        hr = state_ref[k, :, :SCAN_COLS]
        hi = state_ref[k, :, SCAN_COLS:]
        for t in range(TT_SSM):
            rows = slice(t * BATCH, (t + 1) * BATCH)
            nr = ar * hr - ai * hi + hs[rows, :SCAN_COLS]
            ni = ar * hi + ai * hr + hs[rows, SCAN_COLS:]
            hs[rows, :SCAN_COLS] = nr
            hs[rows, SCAN_COLS:] = ni
            hr, hi = nr, ni
        state_ref[k, :, :SCAN_COLS] = hr
        state_ref[k, :, SCAN_COLS:] = hi

    per = SCAN_CHUNKS // (SSM_WIDTH // MXU_DIM)

    def project_out(n):
        acc = None
        for k in range(n * per, (n + 1) * per):
            part = jnp.dot(hs_refs[k][...].astype(BF16), wc_ref[0, k], preferred_element_type=F32)
            acc = part if acc is None else acc + part
        return acc

    ys = []
    project_in(0)
    for k in range(SCAN_CHUNKS):
        if k + 1 < SCAN_CHUNKS:
            project_in(k + 1)
        scan(k)
        if k % per == per - 1:
            ys.append(project_out(k // per))

    u = jnp.concatenate([utb_ref[s] for s in range(n_slab)], axis=1)
    y = jnp.concatenate(ys, axis=1) + d_ref[0] * u
    z = jax.nn.gelu(y)
    gate = jax.nn.sigmoid(jnp.dot(z.astype(BF16), wglu_ref[0], preferred_element_type=F32) + bglu_ref[0])
    res = _rms(z * gate, g_ref[0])
    for s in range(n_slab):
        otb_ref[s] = res[:, s * LANES:(s + 1) * LANES]
    for b in range(BATCH):
        for s in range(n_slab):
            c0 = b * SSM_WIDTH + s * LANES
            o_ref[:, c0:c0 + LANES] = otb_ref[s, pl.ds(b, TT_SSM, stride=BATCH), :].astype(BF16)


def _ssm(layer, u_tm, wb, a_re, a_im, wc, d_skip, w_glu, b_glu, ssm_g):
    rows = TT_SSM * BATCH
    vec = lambda n: pl.BlockSpec((1, 1, n), lambda i: (layer, 0, 0))
    return pl.pallas_call(
        _ssm_kernel,
        grid=(SEQ // TT_SSM,),
        in_specs=[
            pl.BlockSpec((TT_SSM, BATCH * SSM_WIDTH), lambda i: (i, 0)),
            pl.BlockSpec((1,) + wb.shape[1:], lambda i: (layer, 0, 0, 0)),
            vec(STATE_COLS), vec(STATE_COLS),
            pl.BlockSpec((1,) + wc.shape[1:], lambda i: (layer, 0, 0, 0)),
            vec(SSM_WIDTH),
            pl.BlockSpec((1, SSM_WIDTH, SSM_WIDTH), lambda i: (layer, 0, 0)),
            vec(SSM_WIDTH), vec(SSM_WIDTH),
        ],
        out_specs=pl.BlockSpec((TT_SSM, BATCH * SSM_WIDTH), lambda i: (i, 0)),
        out_shape=jax.ShapeDtypeStruct((SEQ, BATCH * SSM_WIDTH), BF16),
        scratch_shapes=[
            pltpu.VMEM((SSM_WIDTH // LANES, rows, LANES), F32),
            pltpu.VMEM((SSM_WIDTH // LANES, rows, LANES), F32),
            pltpu.VMEM((SCAN_CHUNKS, BATCH, 2 * SCAN_COLS), F32),
        ] + [pltpu.VMEM((rows, 2 * SCAN_COLS), F32) for _ in range(SCAN_CHUNKS)],
        compiler_params=pltpu.CompilerParams(
            dimension_semantics=("arbitrary",), vmem_limit_bytes=VMEM_LIMIT),
        name="s5_mixer",
    )(u_tm, wb, a_re, a_im, wc, d_skip, w_glu, b_glu, ssm_g)


def _attn_kernel(q_ref, kvc_ref, kvp_ref, ssm_ref, x_ref, mod_ref, sink_ref, ag_ref, wout_ref,
                 pg_ref, o_ref, kpad_ref, vpad_ref, bias_ref, heads_ref, st_ref, p_ref):
    i = pl.program_id(1)
    nq = TQ_ATT // WINDOW

    @pl.when((pl.program_id(0) == 0) & (i == 0))
    def _():
        j = lax.broadcasted_iota(jnp.int32, (2 * WINDOW, WINDOW), 0)
        r = lax.broadcasted_iota(jnp.int32, (2 * WINDOW, WINDOW), 1)
        diff = WINDOW + r - j
        valid = (diff >= 0) & (diff < WINDOW)
        for h in range(N_Q_HEADS):
            slope = 2.0 ** (-8.0 * (h + 1) / N_Q_HEADS)
            bias = -slope * diff.astype(F32)
            bias_ref[h] = jnp.where(valid, bias, NEG_INF)
            bias_ref[N_Q_HEADS + h] = jnp.where(valid & (j >= WINDOW), bias, NEG_INF)

    lane = lax.broadcasted_iota(jnp.int32, (1, LANES), 1)
    for src, rows in ((kvp_ref, slice(0, WINDOW)), (kvc_ref, slice(WINDOW, WINDOW + TQ_ATT))):
        for hk in range(N_KV_HEADS):
            for par in range(2):
                keep = (lane >= par * HEAD_DIM) & (lane < (par + 1) * HEAD_DIM)
                off = 0 if par == hk else 2 * KV_WIDTH
                kpad_ref[hk * 2 + par, rows, :] = jnp.where(
                    keep, src[0, :, off:off + KV_WIDTH], jnp.zeros((), BF16))
                vpad_ref[hk * 2 + par, rows, :] = jnp.where(
                    keep, src[0, :, off + KV_WIDTH:off + 2 * KV_WIDTH], jnp.zeros((), BF16))

    nt = (((1,), (1,)), ((), ()))
    tn = (((0,), (0,)), ((), ()))

    def scores(jb):
        q0 = jb * WINDOW
        for hk in range(N_KV_HEADS):
            qq = jnp.concatenate(
                [q_ref[0, q0:q0 + WINDOW, (2 * hk + l2) * LANES:(2 * hk + l2 + 1) * LANES]
                 for l2 in range(2)], axis=0)
            for par in range(2):
                kp = kpad_ref[hk * 2 + par, q0:q0 + 2 * WINDOW, :]
                st_ref[(jb % 2) * 2 * N_KV_HEADS + hk * 2 + par] = lax.dot_general(
                    kp, qq, nt, preferred_element_type=F32)

    def softmax(jb):
        table = jnp.where(i == 0, N_Q_HEADS, 0) if jb == 0 else 0
        for hk in range(N_KV_HEADS):
            for par in range(2):
                for l2 in range(2):
                    h = Q_PER_KV * hk + 2 * l2 + par
                    s = (st_ref[(jb % 2) * 2 * N_KV_HEADS + hk * 2 + par, :,
                                l2 * WINDOW:(l2 + 1) * WINDOW] + bias_ref[table + h])
                    sink = sink_ref[0, 0, h]
                    m = jnp.maximum(jnp.max(s, axis=0, keepdims=True), sink)
                    e = jnp.exp(s - m)
                    denom = jnp.sum(e, axis=0, keepdims=True) + jnp.exp(sink - m)
                    p_ref[(jb % 2) * N_Q_HEADS + h] = (e * (1.0 / denom)).astype(BF16)

    def values(jb):
        q0 = jb * WINDOW
        for lt in range(N_Q_HEADS // 2):
            hk = 2 * lt // Q_PER_KV
            acc = None
            for par in range(2):
                vp = vpad_ref[hk * 2 + par, q0:q0 + 2 * WINDOW, :]
                o = lax.dot_general(p_ref[(jb % 2) * N_Q_HEADS + 2 * lt + par], vp, tn,
                                    preferred_element_type=F32)
                acc = o if acc is None else acc + o
            heads_ref[q0:q0 + WINDOW, lt * LANES:(lt + 1) * LANES] = acc

    scores(0)
    for jb in range(nq):
        if jb + 1 < nq:
            scores(jb + 1)
        softmax(jb)
        values(jb)

    attn = _rms(heads_ref[...], ag_ref[0]).astype(BF16)
    mixed = (jnp.dot(attn, wout_ref[0, :ATTN_WIDTH, :], preferred_element_type=F32)
             + jnp.dot(ssm_ref[...], wout_ref[0, ATTN_WIDTH:, :], preferred_element_type=F32))
    mod = mod_ref[0, 0]
    o_ref[0] = x_ref[0] + mod[2:3] * _rms(mixed, pg_ref[0])


def _attn(layer, q, kv, ssm_tm, x, mod, sinks, attn_g, w_out, post_g):
    nt = SEQ // TQ_ATT
    per_tile = TQ_ATT // WINDOW
    vec = lambda n: pl.BlockSpec((1, 1, n), lambda b, i: (layer, 0, 0))
    return pl.pallas_call(
        _attn_kernel,
        grid=(BATCH, nt),
        in_specs=[
            pl.BlockSpec((1, TQ_ATT, ATTN_WIDTH), lambda b, i: (b, i, 0)),
            pl.BlockSpec((1, TQ_ATT, 4 * KV_WIDTH), lambda b, i: (b, i, 0)),
            pl.BlockSpec((1, WINDOW, 4 * KV_WIDTH),
                         lambda b, i: (b, jnp.maximum(i * per_tile - 1, 0), 0)),
            pl.BlockSpec((TQ_ATT, SSM_WIDTH), lambda b, i: (i, b)),
            pl.BlockSpec((1, TQ_ATT, D_MODEL), lambda b, i: (b, i, 0)),
            pl.BlockSpec((1, 1, N_MOD, D_MODEL), lambda b, i: (layer, b, 0, 0)),
            pl.BlockSpec((1, 1, N_Q_HEADS), lambda b, i: (layer, 0, 0), memory_space=pltpu.SMEM),
            vec(ATTN_WIDTH),
            pl.BlockSpec((1, D_MODEL, D_MODEL), lambda b, i: (layer, 0, 0)),
            vec(D_MODEL),
        ],
        out_specs=pl.BlockSpec((1, TQ_ATT, D_MODEL), lambda b, i: (b, i, 0)),
        out_shape=jax.ShapeDtypeStruct((BATCH, SEQ, D_MODEL), F32),
        scratch_shapes=[
            pltpu.VMEM((2 * N_KV_HEADS, WINDOW + TQ_ATT, LANES), BF16),
            pltpu.VMEM((2 * N_KV_HEADS, WINDOW + TQ_ATT, LANES), BF16),
            pltpu.VMEM((2 * N_Q_HEADS, 2 * WINDOW, WINDOW), F32),
            pltpu.VMEM((TQ_ATT, ATTN_WIDTH), F32),
            pltpu.VMEM((2 * 2 * N_KV_HEADS, 2 * WINDOW, 2 * WINDOW), F32),
            pltpu.VMEM((2 * N_Q_HEADS, 2 * WINDOW, WINDOW), BF16),
        ],
        compiler_params=pltpu.CompilerParams(
            dimension_semantics=("arbitrary", "arbitrary"), vmem_limit_bytes=VMEM_LIMIT),
        name="attn_outproj",
    )(q, kv, kv, ssm_tm, x, mod, sinks, attn_g, w_out, post_g)


def _mlp_kernel(x_ref, mod_ref, g_ref, w1_ref, w2_ref, pg_ref, o_ref, acc_ref):
    x = x_ref[0]
    mod = mod_ref[0, 0]
    h = (_rms(x, g_ref[0]) * (1.0 + mod[4:5]) + mod[3:4]).astype(BF16)
    for c in range(D_FF // FF_CHUNK):
        cols = slice(c * FF_CHUNK, (c + 1) * FF_CHUNK)
        f = jnp.dot(h, w1_ref[0, :, cols], preferred_element_type=F32)
        f = jnp.square(jnp.maximum(f, 0.0)).astype(BF16)
        part = jnp.dot(f, w2_ref[0, cols, :], preferred_element_type=F32)
        if c == 0:
            acc_ref[...] = part
        else:
            acc_ref[...] += part
    o_ref[0] = x + mod[5:6] * _rms(acc_ref[...], pg_ref[0])


def _mlp(layer, x, mod, pre_g, w1, w2, post_g):
    nt = SEQ // TM_MLP
    vec = lambda n: pl.BlockSpec((1, 1, n), lambda b, i: (layer, 0, 0))
    return pl.pallas_call(
        _mlp_kernel,
        grid=(BATCH, nt),
        in_specs=[
            pl.BlockSpec((1, TM_MLP, D_MODEL), lambda b, i: (b, i, 0)),
            pl.BlockSpec((1, 1, N_MOD, D_MODEL), lambda b, i: (layer, b, 0, 0)),
            vec(D_MODEL),
            pl.BlockSpec((1, D_MODEL, D_FF), lambda b, i: (layer, 0, 0), pipeline_mode=pl.Buffered(1)),
            pl.BlockSpec((1, D_FF, D_MODEL), lambda b, i: (layer, 0, 0), pipeline_mode=pl.Buffered(1)),
            vec(D_MODEL),
        ],
        out_specs=pl.BlockSpec((1, TM_MLP, D_MODEL), lambda b, i: (b, i, 0)),
        out_shape=jax.ShapeDtypeStruct((BATCH, SEQ, D_MODEL), F32),
        scratch_shapes=[pltpu.VMEM((TM_MLP, D_MODEL), F32)],
        compiler_params=pltpu.CompilerParams(
            dimension_semantics=("arbitrary", "arbitrary"), vmem_limit_bytes=VMEM_LIMIT),
        name="mlp",
    )(x, mod, pre_g, w1, w2, post_g)


def _ssm_params(lam_re, lam_im, log_dt, b_re, b_im, c_re, c_im):
    dt = jnp.exp(log_dt)[..., None]
    mag = jnp.exp(lam_re * dt)
    ang = lam_im * dt
    ab_r = mag * jnp.cos(ang)
    ab_i = mag * jnp.sin(ang)
    nr = ab_r - 1.0
    ni = ab_i
    den = lam_re * lam_re + lam_im * lam_im
    f_r = (nr * lam_re + ni * lam_im) / den
    f_i = (ni * lam_re - nr * lam_im) / den
    bb_r = f_r[..., None] * b_re - f_i[..., None] * b_im
    bb_i = f_r[..., None] * b_im + f_i[..., None] * b_re
    gpt = MXU_DIM // STATE
    gps = LANES // SSM_GROUP
    n_t = STATE_COLS // MXU_DIM
    n_i, g_i, h_i = np.meshgrid(np.arange(n_t), np.arange(gps), np.arange(gpt), indexing='ij')
    sel_b = jnp.asarray(g_i == gpt * (n_i % (gps // gpt)) + h_i, F32)

    def b_tiles(bb):
        bb = bb.reshape(DEPTH, n_t, gpt, STATE, SSM_GROUP)
        w = jnp.einsum('lnhpc,ngh->lngchp', bb, sel_b)
        return w.reshape(DEPTH, n_t, LANES, MXU_DIM)

    wb = jnp.concatenate([b_tiles(bb_r), b_tiles(bb_i)], axis=1).astype(BF16)

    gpc = MXU_DIM // SSM_GROUP
    eye = jnp.eye(gpc, dtype=F32)

    def c_tiles(cc):
        cc = cc.reshape(DEPTH, N_SSM_GROUPS // gpc, gpc, SSM_GROUP, STATE)
        w = jnp.einsum('lngcp,gh->lngphc', cc, eye)
        return w.reshape(DEPTH, N_SSM_GROUPS // gpc, gpc * STATE, MXU_DIM).astype(BF16)

    per = SCAN_CHUNKS // (SSM_WIDTH // MXU_DIM)
    split = lambda w: w.reshape(DEPTH, SSM_WIDTH // MXU_DIM, per, SCAN_COLS, MXU_DIM)
    wc = jnp.concatenate([split(c_tiles(c_re)), split(c_tiles(-c_im))], axis=3)
    wc = wc.reshape(DEPTH, SCAN_CHUNKS, 2 * SCAN_COLS, MXU_DIM)
    a_re = ab_r.reshape(DEPTH, 1, STATE_COLS)
    a_im = ab_i.reshape(DEPTH, 1, STATE_COLS)
    return wb, a_re, a_im, wc


def kernel(x, c, w_ada, b_ada, pre_mix_g, w_in, attn_sinks, lam_re, lam_im, log_dt, b_re, b_im,
           c_re, c_im, d_skip, w_glu, b_glu, attn_out_g, ssm_out_g, w_out, post_mix_g, pre_mlp_g,
           w_mlp_in, w_mlp_out, post_mlp_g):
    row = lambda a: a.reshape(DEPTH, 1, a.shape[-1])
    mod = _modulation(c, w_ada, b_ada).reshape(DEPTH, BATCH, N_MOD, D_MODEL)
    wb, a_re, a_im, wc = _ssm_params(lam_re, lam_im, log_dt, b_re, b_im, c_re, c_im)
    w_in_b = w_in.astype(BF16)
    w_glu_b = w_glu.astype(BF16)
    w_out_b = w_out.astype(BF16)
    w1_b = w_mlp_in.astype(BF16)
    w2_b = w_mlp_out.astype(BF16)
    sinks = row(attn_sinks)
    for layer in range(DEPTH):
        q, kv, u_tm = _inproj(layer, x, mod, row(pre_mix_g), w_in_b)
        ssm_tm = _ssm(layer, u_tm, wb, a_re, a_im, wc,
                      row(d_skip), w_glu_b, row(b_glu), row(ssm_out_g))
        x = _attn(layer, q, kv, ssm_tm, x, mod, sinks,
                  row(attn_out_g), w_out_b, row(post_mix_g))
        x = _mlp(layer, x, mod, row(pre_mlp_g), w1_b, w2_b, row(post_mlp_g))
    return x
```

```python
import functools

import numpy as np
import jax
import jax.numpy as jnp
from jax import lax
from jax.experimental import pallas as pl
from jax.experimental.pallas import tpu as pltpu

D_MODEL = 1024
BATCH = 8
SEQ = 4096
DEPTH = 4
ATTN_WIDTH = 512
SSM_WIDTH = 512
HEAD_DIM = 64
N_Q_HEADS = 8
N_KV_HEADS = 2
Q_PER_KV = 4
KV_WIDTH = 128
WINDOW = 128
SSM_GROUP = 16
N_SSM_GROUPS = 32
STATE = 64
D_FF = 4096
IN_WIDTH = 1280
N_MOD = 6
EPS = 1e-6
NEG_INF = -1e30

LANES = 128
SUBLANES = 8
MXU_DIM = 256

STATE_COLS = N_SSM_GROUPS * STATE
BF16 = jnp.bfloat16
F32 = jnp.float32

VMEM_LIMIT = 56 * 1024 * 1024

TM_IN = 1024
TT_SSM = 64
SSM_SUB = 2
TQ_ATT = 1024
ATT_SUB = 2
TM_MLP = 512
FF_CHUNK = 1024
SCAN_COLS = LANES * STATE // SSM_GROUP
SCAN_CHUNKS = STATE_COLS // SCAN_COLS


def _rms(x, g):
    return x * lax.rsqrt(jnp.mean(x * x, axis=-1, keepdims=True) + EPS) * g


def _dot_f32_lhs(a, w):
    return lax.dot_general(a, w, (((1,), (0,)), ((), ())), preferred_element_type=F32)


def _mod_kernel(c_ref, w_ref, b_ref, o_ref):
    c = c_ref[...]
    ca = (c * jax.nn.sigmoid(c)).astype(BF16)
    o_ref[0] = jnp.dot(ca, w_ref[0].astype(BF16), preferred_element_type=F32) + b_ref[0]


def _modulation(c, w_ada, b_ada):
    nb = 1536
    return pl.pallas_call(
        _mod_kernel,
        grid=(DEPTH, N_MOD * D_MODEL // nb),
        in_specs=[
            pl.BlockSpec((BATCH, D_MODEL), lambda l, j: (0, 0)),
            pl.BlockSpec((1, D_MODEL, nb), lambda l, j: (l, 0, j)),
            pl.BlockSpec((1, 1, nb), lambda l, j: (l, 0, j)),
        ],
        out_specs=pl.BlockSpec((1, BATCH, nb), lambda l, j: (l, 0, j)),
        out_shape=jax.ShapeDtypeStruct((DEPTH, BATCH, N_MOD * D_MODEL), F32),
        compiler_params=pltpu.CompilerParams(
            dimension_semantics=("arbitrary", "arbitrary"), vmem_limit_bytes=VMEM_LIMIT),
        name="adaln_mod",
    )(c, w_ada, b_ada.reshape(DEPTH, 1, N_MOD * D_MODEL))


def _inproj_kernel(x_ref, mod_ref, g_ref, w_ref, q_ref, kv_ref, u_ref):
    x = x_ref[0]
    mod = mod_ref[0, 0]
    h = _rms(x, g_ref[0]) * (1.0 + mod[1:2]) + mod[0:1]
    proj = jnp.dot(h.astype(BF16), w_ref[0], preferred_element_type=F32)
    q_ref[0] = (proj[:, :ATTN_WIDTH] * (HEAD_DIM ** -0.5)).astype(BF16)
    k = proj[:, ATTN_WIDTH:ATTN_WIDTH + KV_WIDTH]
    v = proj[:, ATTN_WIDTH + KV_WIDTH:ATTN_WIDTH + 2 * KV_WIDTH]
    kv_ref[0, :, 0 * KV_WIDTH:1 * KV_WIDTH] = k.astype(BF16)
    kv_ref[0, :, 1 * KV_WIDTH:2 * KV_WIDTH] = v.astype(BF16)
    kv_ref[0, :, 2 * KV_WIDTH:3 * KV_WIDTH] = pltpu.roll(k, HEAD_DIM, 1).astype(BF16)
    kv_ref[0, :, 3 * KV_WIDTH:4 * KV_WIDTH] = pltpu.roll(v, HEAD_DIM, 1).astype(BF16)
    u_ref[...] = proj[:, ATTN_WIDTH + 2 * KV_WIDTH:]


def _inproj(layer, x, mod, pre_g, w_in):
    nt = SEQ // TM_IN
    return pl.pallas_call(
        _inproj_kernel,
        grid=(BATCH, nt),
        in_specs=[
            pl.BlockSpec((1, TM_IN, D_MODEL), lambda b, i: (b, i, 0)),
            pl.BlockSpec((1, 1, N_MOD, D_MODEL), lambda b, i: (layer, b, 0, 0)),
            pl.BlockSpec((1, 1, D_MODEL), lambda b, i: (layer, 0, 0)),
            pl.BlockSpec((1, D_MODEL, IN_WIDTH), lambda b, i: (layer, 0, 0)),
        ],
        out_specs=[
            pl.BlockSpec((1, TM_IN, ATTN_WIDTH), lambda b, i: (b, i, 0)),
            pl.BlockSpec((1, TM_IN, 4 * KV_WIDTH), lambda b, i: (b, i, 0)),
            pl.BlockSpec((TM_IN, SSM_WIDTH), lambda b, i: (i, b)),
        ],
        out_shape=[
            jax.ShapeDtypeStruct((BATCH, SEQ, ATTN_WIDTH), BF16),
            jax.ShapeDtypeStruct((BATCH, SEQ, 4 * KV_WIDTH), BF16),
            jax.ShapeDtypeStruct((SEQ, BATCH * SSM_WIDTH), F32),
        ],
        compiler_params=pltpu.CompilerParams(
            dimension_semantics=("arbitrary", "arbitrary"), vmem_limit_bytes=VMEM_LIMIT),
        name="in_proj",
    )(x, mod, pre_g, w_in)


def _ssm_kernel(u_ref, wb_ref, are_ref, aim_ref, wc_ref, d_ref, wglu_ref, bglu_ref, g_ref, o_ref,
                utb_ref, otb_ref, state_ref, *hs_refs):
    @pl.when(pl.program_id(0) == 0)
    def _():
        state_ref[...] = jnp.zeros_like(state_ref)

    n_slab = SSM_WIDTH // LANES
    tiles_per_half = SCAN_COLS // MXU_DIM
    per = SCAN_CHUNKS // (SSM_WIDTH // MXU_DIM)

    def interleave(sub):
        t0 = sub * TT_SSM
        for b in range(BATCH):
            for s in range(n_slab):
                c0 = b * SSM_WIDTH + s * LANES
                utb_ref[sub * n_slab + s, pl.ds(b, TT_SSM, stride=BATCH), :] = (
                    u_ref[t0:t0 + TT_SSM, c0:c0 + LANES])

    def project_in(sub, k):
        ub = utb_ref[sub * n_slab + k]
        hs = hs_refs[sub * SCAN_CHUNKS + k]
        for part in range(2):
            for j in range(tiles_per_half):
                n = part * (STATE_COLS // MXU_DIM) + k * tiles_per_half + j
                c0 = part * SCAN_COLS + j * MXU_DIM
                hs[:, c0:c0 + MXU_DIM] = _dot_f32_lhs(ub, wb_ref[0, n])

    def scan(sub, k):
        hs = hs_refs[sub * SCAN_CHUNKS + k]
        cols = slice(k * SCAN_COLS, (k + 1) * SCAN_COLS)
        ar = jnp.broadcast_to(are_ref[0, :, cols], (BATCH, SCAN_COLS))
        ai = jnp.broadcast_to(aim_ref[0, :, cols], (BATCH, SCAN_COLS))
        hr = state_ref[k, :, :SCAN_COLS]
        hi = state_ref[k, :, SCAN_COLS:]
        for t in range(TT_SSM):
            rows = slice(t * BATCH, (t + 1) * BATCH)
            nr = ar * hr - ai * hi + hs[rows, :SCAN_COLS]
            ni = ar * hi + ai * hr + hs[rows, SCAN_COLS:]
            hs[rows, :SCAN_COLS] = nr
            hs[rows, SCAN_COLS:] = ni
            hr, hi = nr, ni
        state_ref[k, :, :SCAN_COLS] = hr
        state_ref[k, :, SCAN_COLS:] = hi

    def project_out(sub, n):
        acc = None
        for k in range(n * per, (n + 1) * per):
            part = _dot_f32_lhs(hs_refs[sub * SCAN_CHUNKS + k][...], wc_ref[0, k])
            acc = part if acc is None else acc + part
        return acc

    def finish(sub, ys):
        u = jnp.concatenate([utb_ref[sub * n_slab + s] for s in range(n_slab)], axis=1)
        y = jnp.concatenate(ys, axis=1) + d_ref[0] * u
        z = jax.nn.gelu(y)
        gate = jax.nn.sigmoid(
            jnp.dot(z.astype(BF16), wglu_ref[0], preferred_element_type=F32) + bglu_ref[0])
        res = _rms(z * gate, g_ref[0])
        for s in range(n_slab):
            otb_ref[sub * n_slab + s] = res[:, s * LANES:(s + 1) * LANES]
        t0 = sub * TT_SSM
        for b in range(BATCH):
            for s in range(n_slab):
                c0 = b * SSM_WIDTH + s * LANES
                o_ref[t0:t0 + TT_SSM, c0:c0 + LANES] = (
                    otb_ref[sub * n_slab + s, pl.ds(b, TT_SSM, stride=BATCH), :].astype(BF16))

    pending = None
    for sub in range(SSM_SUB):
        interleave(sub)
    for sub in range(SSM_SUB):
        project_in(sub, 0)
        for k in range(SCAN_CHUNKS):
            if k + 1 < SCAN_CHUNKS:
                project_in(sub, k + 1)
            if pending is not None and k < len(pending[1]):
                pending[2].append(project_out(pending[0], pending[1][k]))
            scan(sub, k)
        if pending is not None:
            finish(pending[0], pending[2])
        pending = (sub, list(range(SSM_WIDTH // MXU_DIM)), [])
    finish(pending[0], [project_out(pending[0], n) for n in pending[1]])


def _ssm(layer, u_tm, wb, a_re, a_im, wc, d_skip, w_glu, b_glu, ssm_g):
    rows = TT_SSM * BATCH
    tt = SSM_SUB * TT_SSM
    vec = lambda n: pl.BlockSpec((1, 1, n), lambda i: (layer, 0, 0))
    return pl.pallas_call(
        _ssm_kernel,
        grid=(SEQ // tt,),
        in_specs=[
            pl.BlockSpec((tt, BATCH * SSM_WIDTH), lambda i: (i, 0)),
            pl.BlockSpec((1,) + wb.shape[1:], lambda i: (layer, 0, 0, 0)),
            vec(STATE_COLS), vec(STATE_COLS),
            pl.BlockSpec((1,) + wc.shape[1:], lambda i: (layer, 0, 0, 0)),
            vec(SSM_WIDTH),
            pl.BlockSpec((1, SSM_WIDTH, SSM_WIDTH), lambda i: (layer, 0, 0)),
            vec(SSM_WIDTH), vec(SSM_WIDTH),
        ],
        out_specs=pl.BlockSpec((tt, BATCH * SSM_WIDTH), lambda i: (i, 0)),
        out_shape=jax.ShapeDtypeStruct((SEQ, BATCH * SSM_WIDTH), BF16),
        scratch_shapes=[
            pltpu.VMEM((SSM_SUB * SSM_WIDTH // LANES, rows, LANES), F32),
            pltpu.VMEM((SSM_SUB * SSM_WIDTH // LANES, rows, LANES), F32),
            pltpu.VMEM((SCAN_CHUNKS, BATCH, 2 * SCAN_COLS), F32),
        ] + [pltpu.VMEM((rows, 2 * SCAN_COLS), F32) for _ in range(SSM_SUB * SCAN_CHUNKS)],
        compiler_params=pltpu.CompilerParams(
            dimension_semantics=("arbitrary",), vmem_limit_bytes=VMEM_LIMIT),
        name="s5_mixer",
    )(u_tm, wb, a_re, a_im, wc, d_skip, w_glu, b_glu, ssm_g)


def _attn_kernel(q_ref, kvc_ref, kvp_ref, ssm_ref, x_ref, mod_ref, sink_ref, ag_ref, wout_ref,
                 pg_ref, o_ref, kpad_ref, vpad_ref, bias_ref, heads_ref, st_ref, p_ref, attn_ref,
                 mixed_ref):
    i = pl.program_id(1)
    nq = TQ_ATT // WINDOW

    @pl.when((pl.program_id(0) == 0) & (i == 0))
    def _():
        j = lax.broadcasted_iota(jnp.int32, (2 * WINDOW, WINDOW), 0)
        r = lax.broadcasted_iota(jnp.int32, (2 * WINDOW, WINDOW), 1)
        diff = WINDOW + r - j
        valid = (diff >= 0) & (diff < WINDOW)
        for h in range(N_Q_HEADS):
            slope = 2.0 ** (-8.0 * (h + 1) / N_Q_HEADS)
            bias = -slope * diff.astype(F32)
            bias_ref[h] = jnp.where(valid, bias, NEG_INF)
            bias_ref[N_Q_HEADS + h] = jnp.where(valid & (j >= WINDOW), bias, NEG_INF)

    lane = lax.broadcasted_iota(jnp.int32, (1, LANES), 1)
    for src, rows in ((kvp_ref, slice(0, WINDOW)), (kvc_ref, slice(WINDOW, WINDOW + TQ_ATT))):
        for hk in range(N_KV_HEADS):
            for par in range(2):
                keep = (lane >= par * HEAD_DIM) & (lane < (par + 1) * HEAD_DIM)
                off = 0 if par == hk else 2 * KV_WIDTH
                kpad_ref[hk * 2 + par, rows, :] = jnp.where(
                    keep, src[0, :, off:off + KV_WIDTH], jnp.zeros((), BF16))
                vpad_ref[hk * 2 + par, rows, :] = jnp.where(
                    keep, src[0, :, off + KV_WIDTH:off + 2 * KV_WIDTH], jnp.zeros((), BF16))

    nt = (((1,), (1,)), ((), ()))
    tn = (((0,), (0,)), ((), ()))

    def scores(jb):
        q0 = jb * WINDOW
        for hk in range(N_KV_HEADS):
            qq = jnp.concatenate(
                [q_ref[0, q0:q0 + WINDOW, (2 * hk + l2) * LANES:(2 * hk + l2 + 1) * LANES]
                 for l2 in range(2)], axis=0)
            for par in range(2):
                kp = kpad_ref[hk * 2 + par, q0:q0 + 2 * WINDOW, :]
                st_ref[(jb % 2) * 2 * N_KV_HEADS + hk * 2 + par] = lax.dot_general(
                    kp, qq, nt, preferred_element_type=F32)

    def softmax(jb):
        table = jnp.where(i == 0, N_Q_HEADS, 0) if jb == 0 else 0
        for hk in range(N_KV_HEADS):
            for par in range(2):
                for l2 in range(2):
                    h = Q_PER_KV * hk + 2 * l2 + par
                    s = (st_ref[(jb % 2) * 2 * N_KV_HEADS + hk * 2 + par, :,
                                l2 * WINDOW:(l2 + 1) * WINDOW] + bias_ref[table + h])
                    sink = sink_ref[0, 0, h]
                    m = jnp.maximum(jnp.max(s, axis=0, keepdims=True), sink)
                    e = jnp.exp(s - m)
                    denom = jnp.sum(e, axis=0, keepdims=True) + jnp.exp(sink - m)
                    p_ref[(jb % 2) * N_Q_HEADS + h] = (e * (1.0 / denom)).astype(BF16)

    def values(jb):
        q0 = jb * WINDOW
        for lt in range(N_Q_HEADS // 2):
            hk = 2 * lt // Q_PER_KV
            acc = None
            for par in range(2):
                vp = vpad_ref[hk * 2 + par, q0:q0 + 2 * WINDOW, :]
                o = lax.dot_general(p_ref[(jb % 2) * N_Q_HEADS + 2 * lt + par], vp, tn,
                                    preferred_element_type=F32)
                acc = o if acc is None else acc + o
            heads_ref[q0:q0 + WINDOW, lt * LANES:(lt + 1) * LANES] = acc

    mod = mod_ref[0, 0]
    gate = mod[2:3] * pg_ref[0]

    def out_proj_pieces(part):
        rows = slice(part * (TQ_ATT // ATT_SUB), (part + 1) * (TQ_ATT // ATT_SUB))
        n_col = D_MODEL // MXU_DIM

        def piece(c):
            if c == 0:
                attn_ref[rows, :] = _rms(heads_ref[rows, :], ag_ref[0]).astype(BF16)
            cols = slice(c * MXU_DIM, (c + 1) * MXU_DIM)
            mixed_ref[rows, cols] = (
                jnp.dot(attn_ref[rows, :], wout_ref[0, :ATTN_WIDTH, cols], preferred_element_type=F32)
                + jnp.dot(ssm_ref[rows, :], wout_ref[0, ATTN_WIDTH:, cols], preferred_element_type=F32))
            if c == n_col - 1:
                o_ref[0, rows, :] = x_ref[0, rows, :] + _rms(mixed_ref[rows, :], gate)

        return [functools.partial(piece, c) for c in range(n_col)]

    per_part = nq // ATT_SUB
    queue = []
    scores(0)
    for jb in range(nq):
        if jb + 1 < nq:
            scores(jb + 1)
        softmax(jb)
        values(jb)
        for _ in range(-(-len(queue) // (per_part - jb % per_part))):
            queue.pop(0)()
        if (jb + 1) % per_part == 0:
            queue.extend(out_proj_pieces(jb // per_part))
    for piece in queue:
        piece()


def _attn(layer, q, kv, ssm_tm, x, mod, sinks, attn_g, w_out, post_g):
    nt = SEQ // TQ_ATT
    per_tile = TQ_ATT // WINDOW
    vec = lambda n: pl.BlockSpec((1, 1, n), lambda b, i: (layer, 0, 0))
    return pl.pallas_call(
        _attn_kernel,
        grid=(BATCH, nt),
        in_specs=[
            pl.BlockSpec((1, TQ_ATT, ATTN_WIDTH), lambda b, i: (b, i, 0)),
            pl.BlockSpec((1, TQ_ATT, 4 * KV_WIDTH), lambda b, i: (b, i, 0)),
            pl.BlockSpec((1, WINDOW, 4 * KV_WIDTH),
                         lambda b, i: (b, jnp.maximum(i * per_tile - 1, 0), 0)),
            pl.BlockSpec((TQ_ATT, SSM_WIDTH), lambda b, i: (i, b)),
            pl.BlockSpec((1, TQ_ATT, D_MODEL), lambda b, i: (b, i, 0)),
            pl.BlockSpec((1, 1, N_MOD, D_MODEL), lambda b, i: (layer, b, 0, 0)),
            pl.BlockSpec((1, 1, N_Q_HEADS), lambda b, i: (layer, 0, 0), memory_space=pltpu.SMEM),
            vec(ATTN_WIDTH),
            pl.BlockSpec((1, D_MODEL, D_MODEL), lambda b, i: (layer, 0, 0)),
            vec(D_MODEL),
        ],
        out_specs=pl.BlockSpec((1, TQ_ATT, D_MODEL), lambda b, i: (b, i, 0)),
        out_shape=jax.ShapeDtypeStruct((BATCH, SEQ, D_MODEL), F32),
        scratch_shapes=[
            pltpu.VMEM((2 * N_KV_HEADS, WINDOW + TQ_ATT, LANES), BF16),
            pltpu.VMEM((2 * N_KV_HEADS, WINDOW + TQ_ATT, LANES), BF16),
            pltpu.VMEM((2 * N_Q_HEADS, 2 * WINDOW, WINDOW), F32),
            pltpu.VMEM((TQ_ATT, ATTN_WIDTH), F32),
            pltpu.VMEM((2 * 2 * N_KV_HEADS, 2 * WINDOW, 2 * WINDOW), F32),
            pltpu.VMEM((2 * N_Q_HEADS, 2 * WINDOW, WINDOW), BF16),
            pltpu.VMEM((TQ_ATT, ATTN_WIDTH), BF16),
            pltpu.VMEM((TQ_ATT, D_MODEL), F32),
        ],
        compiler_params=pltpu.CompilerParams(
            dimension_semantics=("arbitrary", "arbitrary"), vmem_limit_bytes=VMEM_LIMIT),
        name="attn_outproj",
    )(q, kv, kv, ssm_tm, x, mod, sinks, attn_g, w_out, post_g)


def _mlp_kernel(x_ref, mod_ref, g_ref, w1_ref, w2_ref, pg_ref, o_ref, acc_ref):
    x = x_ref[0]
    mod = mod_ref[0, 0]
    h = (_rms(x, g_ref[0]) * (1.0 + mod[4:5]) + mod[3:4]).astype(BF16)
    for c in range(D_FF // FF_CHUNK):
        cols = slice(c * FF_CHUNK, (c + 1) * FF_CHUNK)
        f = jnp.dot(h, w1_ref[0, :, cols], preferred_element_type=F32)
        f = jnp.square(jnp.maximum(f, 0.0)).astype(BF16)
        part = jnp.dot(f, w2_ref[0, cols, :], preferred_element_type=F32)
        if c == 0:
            acc_ref[...] = part
        else:
            acc_ref[...] += part
    o_ref[0] = x + mod[5:6] * _rms(acc_ref[...], pg_ref[0])


def _mlp(layer, x, mod, pre_g, w1, w2, post_g):
    nt = SEQ // TM_MLP
    vec = lambda n: pl.BlockSpec((1, 1, n), lambda b, i: (layer, 0, 0))
    return pl.pallas_call(
        _mlp_kernel,
        grid=(BATCH, nt),
        in_specs=[
            pl.BlockSpec((1, TM_MLP, D_MODEL), lambda b, i: (b, i, 0)),
            pl.BlockSpec((1, 1, N_MOD, D_MODEL), lambda b, i: (layer, b, 0, 0)),
            vec(D_MODEL),
            pl.BlockSpec((1, D_MODEL, D_FF), lambda b, i: (layer, 0, 0), pipeline_mode=pl.Buffered(1)),
            pl.BlockSpec((1, D_FF, D_MODEL), lambda b, i: (layer, 0, 0), pipeline_mode=pl.Buffered(1)),
            vec(D_MODEL),
        ],
        out_specs=pl.BlockSpec((1, TM_MLP, D_MODEL), lambda b, i: (b, i, 0)),
        out_shape=jax.ShapeDtypeStruct((BATCH, SEQ, D_MODEL), F32),
        scratch_shapes=[pltpu.VMEM((TM_MLP, D_MODEL), F32)],
        compiler_params=pltpu.CompilerParams(
            dimension_semantics=("arbitrary", "arbitrary"), vmem_limit_bytes=VMEM_LIMIT),
        name="mlp",
    )(x, mod, pre_g, w1, w2, post_g)


def _ssm_params(lam_re, lam_im, log_dt, b_re, b_im, c_re, c_im):
    dt = jnp.exp(log_dt)[..., None]
    mag = jnp.exp(lam_re * dt)
    ang = lam_im * dt
    ab_r = mag * jnp.cos(ang)
    ab_i = mag * jnp.sin(ang)
    nr = ab_r - 1.0
    ni = ab_i
    den = lam_re * lam_re + lam_im * lam_im
    f_r = (nr * lam_re + ni * lam_im) / den
    f_i = (ni * lam_re - nr * lam_im) / den
    bb_r = f_r[..., None] * b_re - f_i[..., None] * b_im
    bb_i = f_r[..., None] * b_im + f_i[..., None] * b_re
    gpt = MXU_DIM // STATE
    gps = LANES // SSM_GROUP
    n_t = STATE_COLS // MXU_DIM
    n_i, g_i, h_i = np.meshgrid(np.arange(n_t), np.arange(gps), np.arange(gpt), indexing='ij')
    sel_b = jnp.asarray(g_i == gpt * (n_i % (gps // gpt)) + h_i, F32)

    def b_tiles(bb):
        bb = bb.reshape(DEPTH, n_t, gpt, STATE, SSM_GROUP)
        w = jnp.einsum('lnhpc,ngh->lngchp', bb, sel_b)
        return w.reshape(DEPTH, n_t, LANES, MXU_DIM)

    wb = jnp.concatenate([b_tiles(bb_r), b_tiles(bb_i)], axis=1).astype(BF16)

    gpc = MXU_DIM // SSM_GROUP
    eye = jnp.eye(gpc, dtype=F32)

    def c_tiles(cc):
        cc = cc.reshape(DEPTH, N_SSM_GROUPS // gpc, gpc, SSM_GROUP, STATE)
        w = jnp.einsum('lngcp,gh->lngphc', cc, eye)
        return w.reshape(DEPTH, N_SSM_GROUPS // gpc, gpc * STATE, MXU_DIM).astype(BF16)

    per = SCAN_CHUNKS // (SSM_WIDTH // MXU_DIM)
    split = lambda w: w.reshape(DEPTH, SSM_WIDTH // MXU_DIM, per, SCAN_COLS, MXU_DIM)
    wc = jnp.concatenate([split(c_tiles(c_re)), split(c_tiles(-c_im))], axis=3)
    wc = wc.reshape(DEPTH, SCAN_CHUNKS, 2 * SCAN_COLS, MXU_DIM)
    a_re = ab_r.reshape(DEPTH, 1, STATE_COLS)
    a_im = ab_i.reshape(DEPTH, 1, STATE_COLS)
    return wb, a_re, a_im, wc


def kernel(x, c, w_ada, b_ada, pre_mix_g, w_in, attn_sinks, lam_re, lam_im, log_dt, b_re, b_im,
           c_re, c_im, d_skip, w_glu, b_glu, attn_out_g, ssm_out_g, w_out, post_mix_g, pre_mlp_g,
           w_mlp_in, w_mlp_out, post_mlp_g):
    row = lambda a: a.reshape(DEPTH, 1, a.shape[-1])
    mod = _modulation(c, w_ada, b_ada).reshape(DEPTH, BATCH, N_MOD, D_MODEL)
    wb, a_re, a_im, wc = _ssm_params(lam_re, lam_im, log_dt, b_re, b_im, c_re, c_im)
    w_in_b = w_in.astype(BF16)
    w_glu_b = w_glu.astype(BF16)
    w_out_b = w_out.astype(BF16)
    w1_b = w_mlp_in.astype(BF16)
    w2_b = w_mlp_out.astype(BF16)
    sinks = row(attn_sinks)
    for layer in range(DEPTH):
        q, kv, u_tm = _inproj(layer, x, mod, row(pre_mix_g), w_in_b)
        ssm_tm = _ssm(layer, u_tm, wb, a_re, a_im, wc,
                      row(d_skip), w_glu_b, row(b_glu), row(ssm_out_g))
        x = _attn(layer, q, kv, ssm_tm, x, mod, sinks,
                  row(attn_out_g), w_out_b, row(post_mix_g))
        x = _mlp(layer, x, mod, row(pre_mlp_g), w1_b, w2_b, row(post_mlp_g))
    return x
```

```python
import numpy as np
import jax
import jax.numpy as jnp
from jax import lax
from jax.experimental import pallas as pl
from jax.experimental.pallas import tpu as pltpu

D_MODEL = 1024
BATCH = 8
SEQ = 4096
DEPTH = 4
ATTN_WIDTH = 512
SSM_WIDTH = 512
HEAD_DIM = 64
N_Q_HEADS = 8
N_KV_HEADS = 2
Q_PER_KV = 4
KV_WIDTH = 128
WINDOW = 128
SSM_GROUP = 16
N_SSM_GROUPS = 32
STATE = 64
D_FF = 4096
IN_WIDTH = 1280
N_MOD = 6
EPS = 1e-6
NEG_INF = -1e30

LANES = 128
SUBLANES = 8
MXU_DIM = 256

STATE_COLS = N_SSM_GROUPS * STATE
BF16 = jnp.bfloat16
F32 = jnp.float32

VMEM_LIMIT = 56 * 1024 * 1024

TM_IN = 1024
TT_SSM = 64
SSM_SUB = 2
TQ_ATT = 512
FF_CHUNK = 1024
SCAN_COLS = LANES * STATE // SSM_GROUP
SCAN_CHUNKS = STATE_COLS // SCAN_COLS


def _rms(x, g):
    return x * lax.rsqrt(jnp.mean(x * x, axis=-1, keepdims=True) + EPS) * g


def _dot_f32_lhs(a, w):
    return lax.dot_general(a, w, (((1,), (0,)), ((), ())), preferred_element_type=F32)


def _mod_kernel(c_ref, w_ref, b_ref, o_ref):
    c = c_ref[...]
    ca = (c * jax.nn.sigmoid(c)).astype(BF16)
    o_ref[0] = jnp.dot(ca, w_ref[0].astype(BF16), preferred_element_type=F32) + b_ref[0]


def _modulation(c, w_ada, b_ada):
    nb = 1536
    return pl.pallas_call(
        _mod_kernel,
        grid=(DEPTH, N_MOD * D_MODEL // nb),
        in_specs=[
            pl.BlockSpec((BATCH, D_MODEL), lambda l, j: (0, 0)),
            pl.BlockSpec((1, D_MODEL, nb), lambda l, j: (l, 0, j)),
            pl.BlockSpec((1, 1, nb), lambda l, j: (l, 0, j)),
        ],
        out_specs=pl.BlockSpec((1, BATCH, nb), lambda l, j: (l, 0, j)),
        out_shape=jax.ShapeDtypeStruct((DEPTH, BATCH, N_MOD * D_MODEL), F32),
        compiler_params=pltpu.CompilerParams(
            dimension_semantics=("arbitrary", "arbitrary"), vmem_limit_bytes=VMEM_LIMIT),
        name="adaln_mod",
    )(c, w_ada, b_ada.reshape(DEPTH, 1, N_MOD * D_MODEL))


def _inproj_kernel(x_ref, mod_ref, g_ref, w_ref, q_ref, kv_ref, u_ref):
    x = x_ref[0]
    mod = mod_ref[0, 0]
    h = _rms(x, g_ref[0]) * (1.0 + mod[1:2]) + mod[0:1]
    proj = jnp.dot(h.astype(BF16), w_ref[0], preferred_element_type=F32)
    q_ref[0] = (proj[:, :ATTN_WIDTH] * (HEAD_DIM ** -0.5)).astype(BF16)
    k = proj[:, ATTN_WIDTH:ATTN_WIDTH + KV_WIDTH]
    v = proj[:, ATTN_WIDTH + KV_WIDTH:ATTN_WIDTH + 2 * KV_WIDTH]
    kv_ref[0, :, 0 * KV_WIDTH:1 * KV_WIDTH] = k.astype(BF16)
    kv_ref[0, :, 1 * KV_WIDTH:2 * KV_WIDTH] = v.astype(BF16)
    kv_ref[0, :, 2 * KV_WIDTH:3 * KV_WIDTH] = pltpu.roll(k, HEAD_DIM, 1).astype(BF16)
    kv_ref[0, :, 3 * KV_WIDTH:4 * KV_WIDTH] = pltpu.roll(v, HEAD_DIM, 1).astype(BF16)
    u_ref[...] = proj[:, ATTN_WIDTH + 2 * KV_WIDTH:]


def _inproj(layer, x, mod, pre_g, w_in):
    nt = SEQ // TM_IN
    return pl.pallas_call(
        _inproj_kernel,
        grid=(BATCH, nt),
        in_specs=[
            pl.BlockSpec((1, TM_IN, D_MODEL), lambda b, i: (b, i, 0)),
            pl.BlockSpec((1, 1, N_MOD, D_MODEL), lambda b, i: (layer, b, 0, 0)),
            pl.BlockSpec((1, 1, D_MODEL), lambda b, i: (layer, 0, 0)),
            pl.BlockSpec((1, D_MODEL, IN_WIDTH), lambda b, i: (layer, 0, 0)),
        ],
        out_specs=[
            pl.BlockSpec((1, TM_IN, ATTN_WIDTH), lambda b, i: (b, i, 0)),
            pl.BlockSpec((1, TM_IN, 4 * KV_WIDTH), lambda b, i: (b, i, 0)),
            pl.BlockSpec((TM_IN, SSM_WIDTH), lambda b, i: (i, b)),
        ],
        out_shape=[
            jax.ShapeDtypeStruct((BATCH, SEQ, ATTN_WIDTH), BF16),
            jax.ShapeDtypeStruct((BATCH, SEQ, 4 * KV_WIDTH), BF16),
            jax.ShapeDtypeStruct((SEQ, BATCH * SSM_WIDTH), F32),
        ],
        compiler_params=pltpu.CompilerParams(
            dimension_semantics=("arbitrary", "arbitrary"), vmem_limit_bytes=VMEM_LIMIT),
        name="in_proj",
    )(x, mod, pre_g, w_in)


def _ssm_kernel(u_ref, wb_ref, are_ref, aim_ref, wc_ref, d_ref, wglu_ref, bglu_ref, g_ref, o_ref,
                utb_ref, otb_ref, state_ref, *hs_refs):
    @pl.when(pl.program_id(0) == 0)
    def _():
        state_ref[...] = jnp.zeros_like(state_ref)

    n_slab = SSM_WIDTH // LANES
    tiles_per_half = SCAN_COLS // MXU_DIM
    per = SCAN_CHUNKS // (SSM_WIDTH // MXU_DIM)

    def interleave(sub):
        t0 = sub * TT_SSM
        for b in range(BATCH):
            for s in range(n_slab):
                c0 = b * SSM_WIDTH + s * LANES
                utb_ref[sub * n_slab + s, pl.ds(b, TT_SSM, stride=BATCH), :] = (
                    u_ref[t0:t0 + TT_SSM, c0:c0 + LANES])

    def project_in(sub, k):
        ub = utb_ref[sub * n_slab + k]
        hs = hs_refs[sub * SCAN_CHUNKS + k]
        for part in range(2):
            for j in range(tiles_per_half):
                n = part * (STATE_COLS // MXU_DIM) + k * tiles_per_half + j
                c0 = part * SCAN_COLS + j * MXU_DIM
                hs[:, c0:c0 + MXU_DIM] = _dot_f32_lhs(ub, wb_ref[0, n])

    def scan(sub, k):
        hs = hs_refs[sub * SCAN_CHUNKS + k]
        cols = slice(k * SCAN_COLS, (k + 1) * SCAN_COLS)
        ar = jnp.broadcast_to(are_ref[0, :, cols], (BATCH, SCAN_COLS))
        ai = jnp.broadcast_to(aim_ref[0, :, cols], (BATCH, SCAN_COLS))
        hr = state_ref[k, :, :SCAN_COLS]
        hi = state_ref[k, :, SCAN_COLS:]
        for t in range(TT_SSM):
            rows = slice(t * BATCH, (t + 1) * BATCH)
            nr = ar * hr - ai * hi + hs[rows, :SCAN_COLS]
            ni = ar * hi + ai * hr + hs[rows, SCAN_COLS:]
            hs[rows, :SCAN_COLS] = nr
            hs[rows, SCAN_COLS:] = ni
            hr, hi = nr, ni
        state_ref[k, :, :SCAN_COLS] = hr
        state_ref[k, :, SCAN_COLS:] = hi

    def project_out(sub, n):
        acc = None
        for k in range(n * per, (n + 1) * per):
            part = _dot_f32_lhs(hs_refs[sub * SCAN_CHUNKS + k][...], wc_ref[0, k])
            acc = part if acc is None else acc + part
        return acc

    def finish(sub, ys):
        u = jnp.concatenate([utb_ref[sub * n_slab + s] for s in range(n_slab)], axis=1)
        y = jnp.concatenate(ys, axis=1) + d_ref[0] * u
        z = jax.nn.gelu(y)
        gate = jax.nn.sigmoid(
            jnp.dot(z.astype(BF16), wglu_ref[0], preferred_element_type=F32) + bglu_ref[0])
        res = _rms(z * gate, g_ref[0])
        for s in range(n_slab):
            otb_ref[sub * n_slab + s] = res[:, s * LANES:(s + 1) * LANES]
        t0 = sub * TT_SSM
        for b in range(BATCH):
            for s in range(n_slab):
                c0 = b * SSM_WIDTH + s * LANES
                o_ref[t0:t0 + TT_SSM, c0:c0 + LANES] = (
                    otb_ref[sub * n_slab + s, pl.ds(b, TT_SSM, stride=BATCH), :].astype(BF16))

    pending = None
    for sub in range(SSM_SUB):
        interleave(sub)
    for sub in range(SSM_SUB):
        project_in(sub, 0)
        for k in range(SCAN_CHUNKS):
            if k + 1 < SCAN_CHUNKS:
                project_in(sub, k + 1)
            if pending is not None and k < len(pending[1]):
                pending[2].append(project_out(pending[0], pending[1][k]))
            scan(sub, k)
        if pending is not None:
            finish(pending[0], pending[2])
        pending = (sub, list(range(SSM_WIDTH // MXU_DIM)), [])
    finish(pending[0], [project_out(pending[0], n) for n in pending[1]])


def _ssm(layer, u_tm, wb, a_re, a_im, wc, d_skip, w_glu, b_glu, ssm_g):
    rows = TT_SSM * BATCH
    tt = SSM_SUB * TT_SSM
    vec = lambda n: pl.BlockSpec((1, 1, n), lambda i: (layer, 0, 0))
    return pl.pallas_call(
        _ssm_kernel,
        grid=(SEQ // tt,),
        in_specs=[
            pl.BlockSpec((tt, BATCH * SSM_WIDTH), lambda i: (i, 0)),
            pl.BlockSpec((1,) + wb.shape[1:], lambda i: (layer, 0, 0, 0)),
            vec(STATE_COLS), vec(STATE_COLS),
            pl.BlockSpec((1,) + wc.shape[1:], lambda i: (layer, 0, 0, 0)),
            vec(SSM_WIDTH),
            pl.BlockSpec((1, SSM_WIDTH, SSM_WIDTH), lambda i: (layer, 0, 0)),
            vec(SSM_WIDTH), vec(SSM_WIDTH),
        ],
        out_specs=pl.BlockSpec((tt, BATCH * SSM_WIDTH), lambda i: (i, 0)),
        out_shape=jax.ShapeDtypeStruct((SEQ, BATCH * SSM_WIDTH), BF16),
        scratch_shapes=[
            pltpu.VMEM((SSM_SUB * SSM_WIDTH // LANES, rows, LANES), F32),
            pltpu.VMEM((SSM_SUB * SSM_WIDTH // LANES, rows, LANES), F32),
            pltpu.VMEM((SCAN_CHUNKS, BATCH, 2 * SCAN_COLS), F32),
        ] + [pltpu.VMEM((rows, 2 * SCAN_COLS), F32) for _ in range(SSM_SUB * SCAN_CHUNKS)],
        compiler_params=pltpu.CompilerParams(
            dimension_semantics=("arbitrary",), vmem_limit_bytes=VMEM_LIMIT),
        name="s5_mixer",
    )(u_tm, wb, a_re, a_im, wc, d_skip, w_glu, b_glu, ssm_g)


N_TILES = BATCH * (SEQ // TQ_ATT)


def _attn_mlp_kernel(q_ref, kvc_ref, kvp_ref, ssm_ref, x_ref, mod_ref, modp_ref, sink_ref, ag_ref,
                     wout_ref, pg_ref, mg_ref, w1_ref, w2_ref, mpg_ref, o_ref,
                     kpad_ref, vpad_ref, bias_ref, heads_ref, st_ref, p_ref, x1_ref, hmlp_ref,
                     acc_ref):
    s = pl.program_id(0)
    i = jnp.minimum(s, N_TILES - 1) % (SEQ // TQ_ATT)
    nq = TQ_ATT // WINDOW

    @pl.when(s == 0)
    def _():
        j = lax.broadcasted_iota(jnp.int32, (2 * WINDOW, WINDOW), 0)
        r = lax.broadcasted_iota(jnp.int32, (2 * WINDOW, WINDOW), 1)
        diff = WINDOW + r - j
        valid = (diff >= 0) & (diff < WINDOW)
        for h in range(N_Q_HEADS):
            slope = 2.0 ** (-8.0 * (h + 1) / N_Q_HEADS)
            bias = -slope * diff.astype(F32)
            bias_ref[h] = jnp.where(valid, bias, NEG_INF)
            bias_ref[N_Q_HEADS + h] = jnp.where(valid & (j >= WINDOW), bias, NEG_INF)
        x1_ref[...] = jnp.zeros_like(x1_ref)

    lane = lax.broadcasted_iota(jnp.int32, (1, LANES), 1)
    for src, rows in ((kvp_ref, slice(0, WINDOW)), (kvc_ref, slice(WINDOW, WINDOW + TQ_ATT))):
        for hk in range(N_KV_HEADS):
            for par in range(2):
                keep = (lane >= par * HEAD_DIM) & (lane < (par + 1) * HEAD_DIM)
                off = 0 if par == hk else 2 * KV_WIDTH
                kpad_ref[hk * 2 + par, rows, :] = jnp.where(
                    keep, src[0, :, off:off + KV_WIDTH], jnp.zeros((), BF16))
                vpad_ref[hk * 2 + par, rows, :] = jnp.where(
                    keep, src[0, :, off + KV_WIDTH:off + 2 * KV_WIDTH], jnp.zeros((), BF16))

    nt = (((1,), (1,)), ((), ()))
    tn = (((0,), (0,)), ((), ()))

    def scores(jb):
        q0 = jb * WINDOW
        for hk in range(N_KV_HEADS):
            qq = jnp.concatenate(
                [q_ref[0, q0:q0 + WINDOW, (2 * hk + l2) * LANES:(2 * hk + l2 + 1) * LANES]
                 for l2 in range(2)], axis=0)
            for par in range(2):
                kp = kpad_ref[hk * 2 + par, q0:q0 + 2 * WINDOW, :]
                st_ref[(jb % 2) * 2 * N_KV_HEADS + hk * 2 + par] = lax.dot_general(
                    kp, qq, nt, preferred_element_type=F32)

    def softmax(jb):
        table = jnp.where(i == 0, N_Q_HEADS, 0) if jb == 0 else 0
        for hk in range(N_KV_HEADS):
            for par in range(2):
                for l2 in range(2):
                    h = Q_PER_KV * hk + 2 * l2 + par
                    sc = (st_ref[(jb % 2) * 2 * N_KV_HEADS + hk * 2 + par, :,
                                 l2 * WINDOW:(l2 + 1) * WINDOW] + bias_ref[table + h])
                    sink = sink_ref[0, 0, h]
                    m = jnp.maximum(jnp.max(sc, axis=0, keepdims=True), sink)
                    e = jnp.exp(sc - m)
                    denom = jnp.sum(e, axis=0, keepdims=True) + jnp.exp(sink - m)
                    p_ref[(jb % 2) * N_Q_HEADS + h] = (e * (1.0 / denom)).astype(BF16)

    def values(jb):
        q0 = jb * WINDOW
        for lt in range(N_Q_HEADS // 2):
            hk = 2 * lt // Q_PER_KV
            acc = None
            for par in range(2):
                vp = vpad_ref[hk * 2 + par, q0:q0 + 2 * WINDOW, :]
                o = lax.dot_general(p_ref[(jb % 2) * N_Q_HEADS + 2 * lt + par], vp, tn,
                                    preferred_element_type=F32)
                acc = o if acc is None else acc + o
            heads_ref[q0:q0 + WINDOW, lt * LANES:(lt + 1) * LANES] = acc

    modp = modp_ref[0, 0]
    n_ff = D_FF // FF_CHUNK

    def mlp_piece(c):
        if c == 0:
            hmlp_ref[...] = (_rms(x1_ref[...], mg_ref[0]) * (1.0 + modp[4:5]) + modp[3:4]).astype(BF16)
        cols = slice(c * FF_CHUNK, (c + 1) * FF_CHUNK)
        f = jnp.dot(hmlp_ref[...], w1_ref[0, :, cols], preferred_element_type=F32)
        f = jnp.square(jnp.maximum(f, 0.0)).astype(BF16)
        part = jnp.dot(f, w2_ref[0, cols, :], preferred_element_type=F32)
        if c == 0:
            acc_ref[...] = part
        else:
            acc_ref[...] += part
        if c == n_ff - 1:
            o_ref[0] = x1_ref[...] + _rms(acc_ref[...], modp[5:6] * mpg_ref[0])

    pieces = list(range(n_ff))
    scores(0)
    for jb in range(nq):
        if jb + 1 < nq:
            scores(jb + 1)
        softmax(jb)
        values(jb)
        for _ in range(-(-len(pieces) // (nq - jb))):
            mlp_piece(pieces.pop(0))

    mod = mod_ref[0, 0]
    attn = _rms(heads_ref[...], ag_ref[0]).astype(BF16)
    mixed = (jnp.dot(attn, wout_ref[0, :ATTN_WIDTH, :], preferred_element_type=F32)
             + jnp.dot(ssm_ref[...], wout_ref[0, ATTN_WIDTH:, :], preferred_element_type=F32))
    x1_ref[...] = x_ref[0] + _rms(mixed, mod[2:3] * pg_ref[0])


def _attn_mlp(layer, q, kv, ssm_tm, x, mod, sinks, attn_g, w_out, post_g, mlp_g, w1, w2, mlp_post_g):
    per_seq = SEQ // TQ_ATT
    per_tile = TQ_ATT // WINDOW
    cur = lambda s: jnp.minimum(s, N_TILES - 1)
    prev = lambda s: jnp.maximum(s - 1, 0)
    tile = lambda t: (t // per_seq, t % per_seq)
    vec = lambda n: pl.BlockSpec((1, 1, n), lambda s: (layer, 0, 0))
    const = dict(pipeline_mode=pl.Buffered(1))
    return pl.pallas_call(
        _attn_mlp_kernel,
        grid=(N_TILES + 1,),
        in_specs=[
            pl.BlockSpec((1, TQ_ATT, ATTN_WIDTH), lambda s: tile(cur(s)) + (0,)),
            pl.BlockSpec((1, TQ_ATT, 4 * KV_WIDTH), lambda s: tile(cur(s)) + (0,)),
            pl.BlockSpec((1, WINDOW, 4 * KV_WIDTH),
                         lambda s: (cur(s) // per_seq,
                                    jnp.maximum((cur(s) % per_seq) * per_tile - 1, 0), 0)),
            pl.BlockSpec((TQ_ATT, SSM_WIDTH), lambda s: tile(cur(s))[::-1]),
            pl.BlockSpec((1, TQ_ATT, D_MODEL), lambda s: tile(cur(s)) + (0,)),
            pl.BlockSpec((1, 1, N_MOD, D_MODEL), lambda s: (layer, cur(s) // per_seq, 0, 0)),
            pl.BlockSpec((1, 1, N_MOD, D_MODEL), lambda s: (layer, prev(s) // per_seq, 0, 0)),
            pl.BlockSpec((1, 1, N_Q_HEADS), lambda s: (layer, 0, 0), memory_space=pltpu.SMEM),
            vec(ATTN_WIDTH),
            pl.BlockSpec((1, D_MODEL, D_MODEL), lambda s: (layer, 0, 0), **const),
            vec(D_MODEL),
            vec(D_MODEL),
            pl.BlockSpec((1, D_MODEL, D_FF), lambda s: (layer, 0, 0), **const),
            pl.BlockSpec((1, D_FF, D_MODEL), lambda s: (layer, 0, 0), **const),
            vec(D_MODEL),
        ],
        out_specs=pl.BlockSpec((1, TQ_ATT, D_MODEL), lambda s: tile(prev(s)) + (0,)),
        out_shape=jax.ShapeDtypeStruct((BATCH, SEQ, D_MODEL), F32),
        scratch_shapes=[
            pltpu.VMEM((2 * N_KV_HEADS, WINDOW + TQ_ATT, LANES), BF16),
            pltpu.VMEM((2 * N_KV_HEADS, WINDOW + TQ_ATT, LANES), BF16),
            pltpu.VMEM((2 * N_Q_HEADS, 2 * WINDOW, WINDOW), F32),
            pltpu.VMEM((TQ_ATT, ATTN_WIDTH), F32),
            pltpu.VMEM((2 * 2 * N_KV_HEADS, 2 * WINDOW, 2 * WINDOW), F32),
            pltpu.VMEM((2 * N_Q_HEADS, 2 * WINDOW, WINDOW), BF16),
            pltpu.VMEM((TQ_ATT, D_MODEL), F32),
            pltpu.VMEM((TQ_ATT, D_MODEL), BF16),
            pltpu.VMEM((TQ_ATT, D_MODEL), F32),
        ],
        compiler_params=pltpu.CompilerParams(
            dimension_semantics=("arbitrary",), vmem_limit_bytes=VMEM_LIMIT),
        name="attn_mlp",
    )(q, kv, kv, ssm_tm, x, mod, mod, sinks, attn_g, w_out, post_g, mlp_g, w1, w2, mlp_post_g)


def _ssm_params(lam_re, lam_im, log_dt, b_re, b_im, c_re, c_im):
    dt = jnp.exp(log_dt)[..., None]
    mag = jnp.exp(lam_re * dt)
    ang = lam_im * dt
    ab_r = mag * jnp.cos(ang)
    ab_i = mag * jnp.sin(ang)
    nr = ab_r - 1.0
    ni = ab_i
    den = lam_re * lam_re + lam_im * lam_im
    f_r = (nr * lam_re + ni * lam_im) / den
    f_i = (ni * lam_re - nr * lam_im) / den
    bb_r = f_r[..., None] * b_re - f_i[..., None] * b_im
    bb_i = f_r[..., None] * b_im + f_i[..., None] * b_re
    gpt = MXU_DIM // STATE
    gps = LANES // SSM_GROUP
    n_t = STATE_COLS // MXU_DIM
    n_i, g_i, h_i = np.meshgrid(np.arange(n_t), np.arange(gps), np.arange(gpt), indexing='ij')
    sel_b = jnp.asarray(g_i == gpt * (n_i % (gps // gpt)) + h_i, F32)

    def b_tiles(bb):
        bb = bb.reshape(DEPTH, n_t, gpt, STATE, SSM_GROUP)
        w = jnp.einsum('lnhpc,ngh->lngchp', bb, sel_b)
        return w.reshape(DEPTH, n_t, LANES, MXU_DIM)

    wb = jnp.concatenate([b_tiles(bb_r), b_tiles(bb_i)], axis=1).astype(BF16)

    gpc = MXU_DIM // SSM_GROUP
    eye = jnp.eye(gpc, dtype=F32)

    def c_tiles(cc):
        cc = cc.reshape(DEPTH, N_SSM_GROUPS // gpc, gpc, SSM_GROUP, STATE)
        w = jnp.einsum('lngcp,gh->lngphc', cc, eye)
        return w.reshape(DEPTH, N_SSM_GROUPS // gpc, gpc * STATE, MXU_DIM).astype(BF16)

    per = SCAN_CHUNKS // (SSM_WIDTH // MXU_DIM)
    split = lambda w: w.reshape(DEPTH, SSM_WIDTH // MXU_DIM, per, SCAN_COLS, MXU_DIM)
    wc = jnp.concatenate([split(c_tiles(c_re)), split(c_tiles(-c_im))], axis=3)
    wc = wc.reshape(DEPTH, SCAN_CHUNKS, 2 * SCAN_COLS, MXU_DIM)
    a_re = ab_r.reshape(DEPTH, 1, STATE_COLS)
    a_im = ab_i.reshape(DEPTH, 1, STATE_COLS)
    return wb, a_re, a_im, wc


def kernel(x, c, w_ada, b_ada, pre_mix_g, w_in, attn_sinks, lam_re, lam_im, log_dt, b_re, b_im,
           c_re, c_im, d_skip, w_glu, b_glu, attn_out_g, ssm_out_g, w_out, post_mix_g, pre_mlp_g,
           w_mlp_in, w_mlp_out, post_mlp_g):
    row = lambda a: a.reshape(DEPTH, 1, a.shape[-1])
    mod = _modulation(c, w_ada, b_ada).reshape(DEPTH, BATCH, N_MOD, D_MODEL)
    wb, a_re, a_im, wc = _ssm_params(lam_re, lam_im, log_dt, b_re, b_im, c_re, c_im)
    w_in_b = w_in.astype(BF16)
    w_glu_b = w_glu.astype(BF16)
    w_out_b = w_out.astype(BF16)
    w1_b = w_mlp_in.astype(BF16)
    w2_b = w_mlp_out.astype(BF16)
    sinks = row(attn_sinks)
    for layer in range(DEPTH):
        q, kv, u_tm = _inproj(layer, x, mod, row(pre_mix_g), w_in_b)
        ssm_tm = _ssm(layer, u_tm, wb, a_re, a_im, wc,
                      row(d_skip), w_glu_b, row(b_glu), row(ssm_out_g))
        x = _attn_mlp(layer, q, kv, ssm_tm, x, mod, sinks, row(attn_out_g), w_out_b,
                      row(post_mix_g), row(pre_mlp_g), w1_b, w2_b, row(post_mlp_g))
    return x
```

```python
import functools

import numpy as np
import jax
import jax.numpy as jnp
from jax import lax
from jax.experimental import pallas as pl
from jax.experimental.pallas import tpu as pltpu

D_MODEL = 1024
BATCH = 8
SEQ = 4096
DEPTH = 4
ATTN_WIDTH = 512
SSM_WIDTH = 512
HEAD_DIM = 64
N_Q_HEADS = 8
N_KV_HEADS = 2
Q_PER_KV = 4
KV_WIDTH = 128
WINDOW = 128
SSM_GROUP = 16
N_SSM_GROUPS = 32
STATE = 64
D_FF = 4096
IN_WIDTH = 1280
N_MOD = 6
EPS = 1e-6
NEG_INF = -1e30

LANES = 128
SUBLANES = 8
MXU_DIM = 256

STATE_COLS = N_SSM_GROUPS * STATE
BF16 = jnp.bfloat16
F32 = jnp.float32

VMEM_LIMIT = 56 * 1024 * 1024

TM_IN = 1024
IN_SUB = 4
TT_SSM = 64
SSM_SUB = 2
TQ_ATT = 1024
ATT_SUB = 2
TM_MLP = 1024
FF_CHUNK = 1024
SCAN_COLS = LANES * STATE // SSM_GROUP
SCAN_CHUNKS = STATE_COLS // SCAN_COLS


def _rms(x, g):
    return x * lax.rsqrt(jnp.mean(x * x, axis=-1, keepdims=True) + EPS) * g


def _dot_f32_lhs(a, w):
    return lax.dot_general(a, w, (((1,), (0,)), ((), ())), preferred_element_type=F32)


def _mod_kernel(c_ref, w_ref, b_ref, o_ref):
    c = c_ref[...]
    ca = (c * jax.nn.sigmoid(c)).astype(BF16)
    o_ref[0] = jnp.dot(ca, w_ref[0].astype(BF16), preferred_element_type=F32) + b_ref[0]


def _modulation(c, w_ada, b_ada):
    nb = 1536
    return pl.pallas_call(
        _mod_kernel,
        grid=(DEPTH, N_MOD * D_MODEL // nb),
        in_specs=[
            pl.BlockSpec((BATCH, D_MODEL), lambda l, j: (0, 0)),
            pl.BlockSpec((1, D_MODEL, nb), lambda l, j: (l, 0, j)),
            pl.BlockSpec((1, 1, nb), lambda l, j: (l, 0, j)),
        ],
        out_specs=pl.BlockSpec((1, BATCH, nb), lambda l, j: (l, 0, j)),
        out_shape=jax.ShapeDtypeStruct((DEPTH, BATCH, N_MOD * D_MODEL), F32),
        compiler_params=pltpu.CompilerParams(
            dimension_semantics=("arbitrary", "arbitrary"), vmem_limit_bytes=VMEM_LIMIT),
        name="adaln_mod",
    )(c, w_ada, b_ada.reshape(DEPTH, 1, N_MOD * D_MODEL))


def _inproj_kernel(x_ref, mod_ref, g_ref, w_ref, q_ref, kv_ref, u_ref, wb_ref):
    @pl.when((pl.program_id(0) == 0) & (pl.program_id(1) == 0))
    def _():
        wb_ref[...] = w_ref[0].astype(BF16)

    mod = mod_ref[0, 0]
    scale = g_ref[0] * (1.0 + mod[1:2])
    for part in range(IN_SUB):
        rows = slice(part * (TM_IN // IN_SUB), (part + 1) * (TM_IN // IN_SUB))
        h = _rms(x_ref[0, rows, :], scale) + mod[0:1]
        proj = jnp.dot(h.astype(BF16), wb_ref[...], preferred_element_type=F32)
        q_ref[0, rows, :] = (proj[:, :ATTN_WIDTH] * (HEAD_DIM ** -0.5)).astype(BF16)
        k = proj[:, ATTN_WIDTH:ATTN_WIDTH + KV_WIDTH]
        v = proj[:, ATTN_WIDTH + KV_WIDTH:ATTN_WIDTH + 2 * KV_WIDTH]
        kv_ref[0, rows, 0 * KV_WIDTH:1 * KV_WIDTH] = k.astype(BF16)
        kv_ref[0, rows, 1 * KV_WIDTH:2 * KV_WIDTH] = v.astype(BF16)
        kv_ref[0, rows, 2 * KV_WIDTH:3 * KV_WIDTH] = pltpu.roll(k, HEAD_DIM, 1).astype(BF16)
        kv_ref[0, rows, 3 * KV_WIDTH:4 * KV_WIDTH] = pltpu.roll(v, HEAD_DIM, 1).astype(BF16)
        u_ref[rows, :] = proj[:, ATTN_WIDTH + 2 * KV_WIDTH:]


def _inproj(layer, x, mod, pre_g, w_in):
    nt = SEQ // TM_IN
    return pl.pallas_call(
        _inproj_kernel,
        grid=(BATCH, nt),
        in_specs=[
            pl.BlockSpec((1, TM_IN, D_MODEL), lambda b, i: (b, i, 0)),
            pl.BlockSpec((1, 1, N_MOD, D_MODEL), lambda b, i: (layer, b, 0, 0)),
            pl.BlockSpec((1, 1, D_MODEL), lambda b, i: (layer, 0, 0)),
            pl.BlockSpec((1, D_MODEL, IN_WIDTH), lambda b, i: (layer, 0, 0)),
        ],
        out_specs=[
            pl.BlockSpec((1, TM_IN, ATTN_WIDTH), lambda b, i: (b, i, 0)),
            pl.BlockSpec((1, TM_IN, 4 * KV_WIDTH), lambda b, i: (b, i, 0)),
            pl.BlockSpec((TM_IN, SSM_WIDTH), lambda b, i: (i, b)),
        ],
        out_shape=[
            jax.ShapeDtypeStruct((BATCH, SEQ, ATTN_WIDTH), BF16),
            jax.ShapeDtypeStruct((BATCH, SEQ, 4 * KV_WIDTH), BF16),
            jax.ShapeDtypeStruct((SEQ, BATCH * SSM_WIDTH), F32),
        ],
        scratch_shapes=[pltpu.VMEM((D_MODEL, IN_WIDTH), BF16)],
        compiler_params=pltpu.CompilerParams(
            dimension_semantics=("arbitrary", "arbitrary"), vmem_limit_bytes=VMEM_LIMIT),
        name="in_proj",
    )(x, mod, pre_g, w_in)


def _ssm_kernel(u_ref, wb_ref, are_ref, aim_ref, wc_ref, d_ref, wglu_ref, bglu_ref, g_ref, o_ref,
                utb_ref, otb_ref, state_ref, wglub_ref, *hs_refs):
    @pl.when(pl.program_id(0) == 0)
    def _():
        state_ref[...] = jnp.zeros_like(state_ref)
        wglub_ref[...] = wglu_ref[0].astype(BF16)

    n_slab = SSM_WIDTH // LANES
    tiles_per_half = SCAN_COLS // MXU_DIM
    per = SCAN_CHUNKS // (SSM_WIDTH // MXU_DIM)

    def interleave(sub):
        t0 = sub * TT_SSM
        for b in range(BATCH):
            for s in range(n_slab):
                c0 = b * SSM_WIDTH + s * LANES
                utb_ref[sub * n_slab + s, pl.ds(b, TT_SSM, stride=BATCH), :] = (
                    u_ref[t0:t0 + TT_SSM, c0:c0 + LANES])

    def project_in(sub, k):
        ub = utb_ref[sub * n_slab + k]
        hs = hs_refs[sub * SCAN_CHUNKS + k]
        for part in range(2):
            for j in range(tiles_per_half):
                n = part * (STATE_COLS // MXU_DIM) + k * tiles_per_half + j
                c0 = part * SCAN_COLS + j * MXU_DIM
                hs[:, c0:c0 + MXU_DIM] = _dot_f32_lhs(ub, wb_ref[0, n])

    def scan(sub, k):
        hs = hs_refs[sub * SCAN_CHUNKS + k]
        cols = slice(k * SCAN_COLS, (k + 1) * SCAN_COLS)
        ar = jnp.broadcast_to(are_ref[0, :, cols], (BATCH, SCAN_COLS))
        ai = jnp.broadcast_to(aim_ref[0, :, cols], (BATCH, SCAN_COLS))
        hr = state_ref[k, :, :SCAN_COLS]
        hi = state_ref[k, :, SCAN_COLS:]
        for t in range(TT_SSM):
            rows = slice(t * BATCH, (t + 1) * BATCH)
            nr = ar * hr - ai * hi + hs[rows, :SCAN_COLS]
            ni = ar * hi + ai * hr + hs[rows, SCAN_COLS:]
            hs[rows, :SCAN_COLS] = nr
            hs[rows, SCAN_COLS:] = ni
            hr, hi = nr, ni
        state_ref[k, :, :SCAN_COLS] = hr
        state_ref[k, :, SCAN_COLS:] = hi

    def project_out(sub, n):
        acc = None
        for k in range(n * per, (n + 1) * per):
            part = _dot_f32_lhs(hs_refs[sub * SCAN_CHUNKS + k][...], wc_ref[0, k])
            acc = part if acc is None else acc + part
        return acc

    def finish(sub, ys):
        u = jnp.concatenate([utb_ref[sub * n_slab + s] for s in range(n_slab)], axis=1)
        y = jnp.concatenate(ys, axis=1) + d_ref[0] * u
        z = jax.nn.gelu(y)
        gate = jax.nn.sigmoid(
            jnp.dot(z.astype(BF16), wglub_ref[...], preferred_element_type=F32) + bglu_ref[0])
        res = _rms(z * gate, g_ref[0])
        for s in range(n_slab):
            otb_ref[sub * n_slab + s] = res[:, s * LANES:(s + 1) * LANES]
        t0 = sub * TT_SSM
        for b in range(BATCH):
            for s in range(n_slab):
                c0 = b * SSM_WIDTH + s * LANES
                o_ref[t0:t0 + TT_SSM, c0:c0 + LANES] = (
                    otb_ref[sub * n_slab + s, pl.ds(b, TT_SSM, stride=BATCH), :].astype(BF16))

    pending = None
    for sub in range(SSM_SUB):
        interleave(sub)
    for sub in range(SSM_SUB):
        project_in(sub, 0)
        for k in range(SCAN_CHUNKS):
            if k + 1 < SCAN_CHUNKS:
                project_in(sub, k + 1)
            if pending is not None and k < len(pending[1]):
                pending[2].append(project_out(pending[0], pending[1][k]))
            scan(sub, k)
        if pending is not None:
            finish(pending[0], pending[2])
        pending = (sub, list(range(SSM_WIDTH // MXU_DIM)), [])
    finish(pending[0], [project_out(pending[0], n) for n in pending[1]])


def _ssm(layer, u_tm, wb, a_re, a_im, wc, d_skip, w_glu, b_glu, ssm_g):
    rows = TT_SSM * BATCH
    tt = SSM_SUB * TT_SSM
    vec = lambda n: pl.BlockSpec((1, 1, n), lambda i: (layer, 0, 0))
    return pl.pallas_call(
        _ssm_kernel,
        grid=(SEQ // tt,),
        in_specs=[
            pl.BlockSpec((tt, BATCH * SSM_WIDTH), lambda i: (i, 0)),
            pl.BlockSpec((1,) + wb.shape[1:], lambda i: (layer, 0, 0, 0)),
            vec(STATE_COLS), vec(STATE_COLS),
            pl.BlockSpec((1,) + wc.shape[1:], lambda i: (layer, 0, 0, 0)),
            vec(SSM_WIDTH),
            pl.BlockSpec((1, SSM_WIDTH, SSM_WIDTH), lambda i: (layer, 0, 0)),
            vec(SSM_WIDTH), vec(SSM_WIDTH),
        ],
        out_specs=pl.BlockSpec((tt, BATCH * SSM_WIDTH), lambda i: (i, 0)),
        out_shape=jax.ShapeDtypeStruct((SEQ, BATCH * SSM_WIDTH), BF16),
        scratch_shapes=[
            pltpu.VMEM((SSM_SUB * SSM_WIDTH // LANES, rows, LANES), F32),
            pltpu.VMEM((SSM_SUB * SSM_WIDTH // LANES, rows, LANES), F32),
            pltpu.VMEM((SCAN_CHUNKS, BATCH, 2 * SCAN_COLS), F32),
            pltpu.VMEM((SSM_WIDTH, SSM_WIDTH), BF16),
        ] + [pltpu.VMEM((rows, 2 * SCAN_COLS), F32) for _ in range(SSM_SUB * SCAN_CHUNKS)],
        compiler_params=pltpu.CompilerParams(
            dimension_semantics=("arbitrary",), vmem_limit_bytes=VMEM_LIMIT),
        name="s5_mixer",
    )(u_tm, wb, a_re, a_im, wc, d_skip, w_glu, b_glu, ssm_g)


def _attn_kernel(q_ref, kvc_ref, kvp_ref, ssm_ref, x_ref, mod_ref, sink_ref, ag_ref, wout_ref,
                 pg_ref, o_ref, kpad_ref, vpad_ref, bias_ref, heads_ref, st_ref, p_ref, attn_ref,
                 mixed_ref, woutb_ref):
    i = pl.program_id(1)
    nq = TQ_ATT // WINDOW

    @pl.when((pl.program_id(0) == 0) & (i == 0))
    def _():
        j = lax.broadcasted_iota(jnp.int32, (2 * WINDOW, WINDOW), 0)
        r = lax.broadcasted_iota(jnp.int32, (2 * WINDOW, WINDOW), 1)
        diff = WINDOW + r - j
        valid = (diff >= 0) & (diff < WINDOW)
        for h in range(N_Q_HEADS):
            slope = 2.0 ** (-8.0 * (h + 1) / N_Q_HEADS)
            bias = -slope * diff.astype(F32)
            bias_ref[h] = jnp.where(valid, bias, NEG_INF)
            bias_ref[N_Q_HEADS + h] = jnp.where(valid & (j >= WINDOW), bias, NEG_INF)
        woutb_ref[...] = wout_ref[0].astype(BF16)

    lane = lax.broadcasted_iota(jnp.int32, (1, LANES), 1)
    for src, rows in ((kvp_ref, slice(0, WINDOW)), (kvc_ref, slice(WINDOW, WINDOW + TQ_ATT))):
        for hk in range(N_KV_HEADS):
            for par in range(2):
                keep = (lane >= par * HEAD_DIM) & (lane < (par + 1) * HEAD_DIM)
                off = 0 if par == hk else 2 * KV_WIDTH
                kpad_ref[hk * 2 + par, rows, :] = jnp.where(
                    keep, src[0, :, off:off + KV_WIDTH], jnp.zeros((), BF16))
                vpad_ref[hk * 2 + par, rows, :] = jnp.where(
                    keep, src[0, :, off + KV_WIDTH:off + 2 * KV_WIDTH], jnp.zeros((), BF16))

    nt = (((1,), (1,)), ((), ()))
    tn = (((0,), (0,)), ((), ()))

    def scores(jb):
        q0 = jb * WINDOW
        for hk in range(N_KV_HEADS):
            qq = jnp.concatenate(
                [q_ref[0, q0:q0 + WINDOW, (2 * hk + l2) * LANES:(2 * hk + l2 + 1) * LANES]
                 for l2 in range(2)], axis=0)
            for par in range(2):
                kp = kpad_ref[hk * 2 + par, q0:q0 + 2 * WINDOW, :]
                st_ref[(jb % 2) * 2 * N_KV_HEADS + hk * 2 + par] = lax.dot_general(
                    kp, qq, nt, preferred_element_type=F32)

    def softmax(jb):
        table = jnp.where(i == 0, N_Q_HEADS, 0) if jb == 0 else 0
        for hk in range(N_KV_HEADS):
            for par in range(2):
                for l2 in range(2):
                    h = Q_PER_KV * hk + 2 * l2 + par
                    s = (st_ref[(jb % 2) * 2 * N_KV_HEADS + hk * 2 + par, :,
                                l2 * WINDOW:(l2 + 1) * WINDOW] + bias_ref[table + h])
                    sink = sink_ref[0, 0, h]
                    m = jnp.maximum(jnp.max(s, axis=0, keepdims=True), sink)
                    e = jnp.exp(s - m)
                    denom = jnp.sum(e, axis=0, keepdims=True) + jnp.exp(sink - m)
                    p_ref[(jb % 2) * N_Q_HEADS + h] = (e * (1.0 / denom)).astype(BF16)

    def values(jb):
        q0 = jb * WINDOW
        for lt in range(N_Q_HEADS // 2):
            hk = 2 * lt // Q_PER_KV
            acc = None
            for par in range(2):
                vp = vpad_ref[hk * 2 + par, q0:q0 + 2 * WINDOW, :]
                o = lax.dot_general(p_ref[(jb % 2) * N_Q_HEADS + 2 * lt + par], vp, tn,
                                    preferred_element_type=F32)
                acc = o if acc is None else acc + o
            heads_ref[q0:q0 + WINDOW, lt * LANES:(lt + 1) * LANES] = acc

    mod = mod_ref[0, 0]
    gate = mod[2:3] * pg_ref[0]

    def out_proj_pieces(part):
        rows = slice(part * (TQ_ATT // ATT_SUB), (part + 1) * (TQ_ATT // ATT_SUB))
        n_col = D_MODEL // MXU_DIM

        def piece(c):
            if c == 0:
                attn_ref[rows, :] = _rms(heads_ref[rows, :], ag_ref[0]).astype(BF16)
            cols = slice(c * MXU_DIM, (c + 1) * MXU_DIM)
            mixed_ref[rows, cols] = (
                jnp.dot(attn_ref[rows, :], woutb_ref[:ATTN_WIDTH, cols], preferred_element_type=F32)
                + jnp.dot(ssm_ref[rows, :], woutb_ref[ATTN_WIDTH:, cols], preferred_element_type=F32))
            if c == n_col - 1:
                o_ref[0, rows, :] = x_ref[0, rows, :] + _rms(mixed_ref[rows, :], gate)

        return [functools.partial(piece, c) for c in range(n_col)]

    per_part = nq // ATT_SUB
    queue = []
    scores(0)
    for jb in range(nq):
        if jb + 1 < nq:
            scores(jb + 1)
        softmax(jb)
        values(jb)
        for _ in range(-(-len(queue) // (per_part - jb % per_part))):
            queue.pop(0)()
        if (jb + 1) % per_part == 0:
            queue.extend(out_proj_pieces(jb // per_part))
    for piece in queue:
        piece()


def _attn(layer, q, kv, ssm_tm, x, mod, sinks, attn_g, w_out, post_g):
    nt = SEQ // TQ_ATT
    per_tile = TQ_ATT // WINDOW
    vec = lambda n: pl.BlockSpec((1, 1, n), lambda b, i: (layer, 0, 0))
    return pl.pallas_call(
        _attn_kernel,
        grid=(BATCH, nt),
        in_specs=[
            pl.BlockSpec((1, TQ_ATT, ATTN_WIDTH), lambda b, i: (b, i, 0)),
            pl.BlockSpec((1, TQ_ATT, 4 * KV_WIDTH), lambda b, i: (b, i, 0)),
            pl.BlockSpec((1, WINDOW, 4 * KV_WIDTH),
                         lambda b, i: (b, jnp.maximum(i * per_tile - 1, 0), 0)),
            pl.BlockSpec((TQ_ATT, SSM_WIDTH), lambda b, i: (i, b)),
            pl.BlockSpec((1, TQ_ATT, D_MODEL), lambda b, i: (b, i, 0)),
            pl.BlockSpec((1, 1, N_MOD, D_MODEL), lambda b, i: (layer, b, 0, 0)),
            pl.BlockSpec((1, 1, N_Q_HEADS), lambda b, i: (layer, 0, 0), memory_space=pltpu.SMEM),
            vec(ATTN_WIDTH),
            pl.BlockSpec((1, D_MODEL, D_MODEL), lambda b, i: (layer, 0, 0)),
            vec(D_MODEL),
        ],
        out_specs=pl.BlockSpec((1, TQ_ATT, D_MODEL), lambda b, i: (b, i, 0)),
        out_shape=jax.ShapeDtypeStruct((BATCH, SEQ, D_MODEL), F32),
        scratch_shapes=[
            pltpu.VMEM((2 * N_KV_HEADS, WINDOW + TQ_ATT, LANES), BF16),
            pltpu.VMEM((2 * N_KV_HEADS, WINDOW + TQ_ATT, LANES), BF16),
            pltpu.VMEM((2 * N_Q_HEADS, 2 * WINDOW, WINDOW), F32),
            pltpu.VMEM((TQ_ATT, ATTN_WIDTH), F32),
            pltpu.VMEM((2 * 2 * N_KV_HEADS, 2 * WINDOW, 2 * WINDOW), F32),
            pltpu.VMEM((2 * N_Q_HEADS, 2 * WINDOW, WINDOW), BF16),
            pltpu.VMEM((TQ_ATT, ATTN_WIDTH), BF16),
            pltpu.VMEM((TQ_ATT, D_MODEL), F32),
            pltpu.VMEM((D_MODEL, D_MODEL), BF16),
        ],
        compiler_params=pltpu.CompilerParams(
            dimension_semantics=("arbitrary", "arbitrary"), vmem_limit_bytes=VMEM_LIMIT),
        name="attn_outproj",
    )(q, kv, kv, ssm_tm, x, mod, sinks, attn_g, w_out, post_g)


def _mlp_kernel(x_ref, mod_ref, g_ref, w1_ref, w2_ref, pg_ref, o_ref, acc_ref):
    x = x_ref[0]
    mod = mod_ref[0, 0]
    h = (_rms(x, g_ref[0] * (1.0 + mod[4:5])) + mod[3:4]).astype(BF16)
    for c in range(D_FF // FF_CHUNK):
        cols = slice(c * FF_CHUNK, (c + 1) * FF_CHUNK)
        f = jnp.dot(h, w1_ref[0, :, cols], preferred_element_type=F32)
        f = jnp.square(jnp.maximum(f, 0.0)).astype(BF16)
        part = jnp.dot(f, w2_ref[0, cols, :], preferred_element_type=F32)
        if c == 0:
            acc_ref[...] = part
        else:
            acc_ref[...] += part
    o_ref[0] = x + _rms(acc_ref[...], mod[5:6] * pg_ref[0])


def _mlp(layer, x, mod, pre_g, w1, w2, post_g):
    nt = SEQ // TM_MLP
    vec = lambda n: pl.BlockSpec((1, 1, n), lambda b, i: (layer, 0, 0))
    return pl.pallas_call(
        _mlp_kernel,
        grid=(BATCH, nt),
        in_specs=[
            pl.BlockSpec((1, TM_MLP, D_MODEL), lambda b, i: (b, i, 0)),
            pl.BlockSpec((1, 1, N_MOD, D_MODEL), lambda b, i: (layer, b, 0, 0)),
            vec(D_MODEL),
            pl.BlockSpec((1, D_MODEL, D_FF), lambda b, i: (layer, 0, 0), pipeline_mode=pl.Buffered(1)),
            pl.BlockSpec((1, D_FF, D_MODEL), lambda b, i: (layer, 0, 0), pipeline_mode=pl.Buffered(1)),
            vec(D_MODEL),
        ],
        out_specs=pl.BlockSpec((1, TM_MLP, D_MODEL), lambda b, i: (b, i, 0)),
        out_shape=jax.ShapeDtypeStruct((BATCH, SEQ, D_MODEL), F32),
        scratch_shapes=[pltpu.VMEM((TM_MLP, D_MODEL), F32)],
        compiler_params=pltpu.CompilerParams(
            dimension_semantics=("arbitrary", "arbitrary"), vmem_limit_bytes=VMEM_LIMIT),
        name="mlp",
    )(x, mod, pre_g, w1, w2, post_g)


def _ssm_params(lam_re, lam_im, log_dt, b_re, b_im, c_re, c_im):
    dt = jnp.exp(log_dt)[..., None]
    mag = jnp.exp(lam_re * dt)
    ang = lam_im * dt
    ab_r = mag * jnp.cos(ang)
    ab_i = mag * jnp.sin(ang)
    nr = ab_r - 1.0
    ni = ab_i
    den = lam_re * lam_re + lam_im * lam_im
    f_r = (nr * lam_re + ni * lam_im) / den
    f_i = (ni * lam_re - nr * lam_im) / den
    bb_r = f_r[..., None] * b_re - f_i[..., None] * b_im
    bb_i = f_r[..., None] * b_im + f_i[..., None] * b_re
    gpt = MXU_DIM // STATE
    gps = LANES // SSM_GROUP
    n_t = STATE_COLS // MXU_DIM
    n_i, g_i, h_i = np.meshgrid(np.arange(n_t), np.arange(gps), np.arange(gpt), indexing='ij')
    sel_b = jnp.asarray(g_i == gpt * (n_i % (gps // gpt)) + h_i, F32)

    def b_tiles(bb):
        bb = bb.reshape(DEPTH, n_t, gpt, STATE, SSM_GROUP)
        w = jnp.einsum('lnhpc,ngh->lngchp', bb, sel_b)
        return w.reshape(DEPTH, n_t, LANES, MXU_DIM)

    wb = jnp.concatenate([b_tiles(bb_r), b_tiles(bb_i)], axis=1).astype(BF16)

    gpc = MXU_DIM // SSM_GROUP
    eye = jnp.eye(gpc, dtype=F32)

    def c_tiles(cc):
        cc = cc.reshape(DEPTH, N_SSM_GROUPS // gpc, gpc, SSM_GROUP, STATE)
        w = jnp.einsum('lngcp,gh->lngphc', cc, eye)
        return w.reshape(DEPTH, N_SSM_GROUPS // gpc, gpc * STATE, MXU_DIM).astype(BF16)

    per = SCAN_CHUNKS // (SSM_WIDTH // MXU_DIM)
    split = lambda w: w.reshape(DEPTH, SSM_WIDTH // MXU_DIM, per, SCAN_COLS, MXU_DIM)
    wc = jnp.concatenate([split(c_tiles(c_re)), split(c_tiles(-c_im))], axis=3)
    wc = wc.reshape(DEPTH, SCAN_CHUNKS, 2 * SCAN_COLS, MXU_DIM)
    a_re = ab_r.reshape(DEPTH, 1, STATE_COLS)
    a_im = ab_i.reshape(DEPTH, 1, STATE_COLS)
    return wb, a_re, a_im, wc


def kernel(x, c, w_ada, b_ada, pre_mix_g, w_in, attn_sinks, lam_re, lam_im, log_dt, b_re, b_im,
           c_re, c_im, d_skip, w_glu, b_glu, attn_out_g, ssm_out_g, w_out, post_mix_g, pre_mlp_g,
           w_mlp_in, w_mlp_out, post_mlp_g):
    row = lambda a: a.reshape(DEPTH, 1, a.shape[-1])
    mod = _modulation(c, w_ada, b_ada).reshape(DEPTH, BATCH, N_MOD, D_MODEL)
    wb, a_re, a_im, wc = _ssm_params(lam_re, lam_im, log_dt, b_re, b_im, c_re, c_im)
    w1_b = w_mlp_in.astype(BF16)
    w2_b = w_mlp_out.astype(BF16)
    sinks = row(attn_sinks)
    for layer in range(DEPTH):
        q, kv, u_tm = _inproj(layer, x, mod, row(pre_mix_g), w_in)
        ssm_tm = _ssm(layer, u_tm, wb, a_re, a_im, wc,
                      row(d_skip), w_glu, row(b_glu), row(ssm_out_g))
        x = _attn(layer, q, kv, ssm_tm, x, mod, sinks,
                  row(attn_out_g), w_out, row(post_mix_g))
        x = _mlp(layer, x, mod, row(pre_mlp_g), w1_b, w2_b, row(post_mlp_g))
    return x
```

```python
import functools

import numpy as np
import jax
import jax.numpy as jnp
from jax import lax
from jax.experimental import pallas as pl
from jax.experimental.pallas import tpu as pltpu

D_MODEL = 1024
BATCH = 8
SEQ = 4096
DEPTH = 4
ATTN_WIDTH = 512
SSM_WIDTH = 512
HEAD_DIM = 64
N_Q_HEADS = 8
N_KV_HEADS = 2
Q_PER_KV = 4
KV_WIDTH = 128
WINDOW = 128
SSM_GROUP = 16
N_SSM_GROUPS = 32
STATE = 64
D_FF = 4096
IN_WIDTH = 1280
N_MOD = 6
EPS = 1e-6
NEG_INF = -1e30

LANES = 128
SUBLANES = 8
MXU_DIM = 256

STATE_COLS = N_SSM_GROUPS * STATE
BF16 = jnp.bfloat16
F32 = jnp.float32

VMEM_LIMIT = 56 * 1024 * 1024

TT_SSM = 64
SSM_SUB = 2
TQ_ATT = 1024
ATT_SUB = 2
TM_MLP = 1024
FF_CHUNK = 1024
SCAN_COLS = LANES * STATE // SSM_GROUP
SCAN_CHUNKS = STATE_COLS // SCAN_COLS


def _rms(x, g):
    return x * lax.rsqrt(jnp.mean(x * x, axis=-1, keepdims=True) + EPS) * g


def _dot_f32_lhs(a, w):
    return lax.dot_general(a, w, (((1,), (0,)), ((), ())), preferred_element_type=F32)


def _mod_kernel(c_ref, w_ref, b_ref, o_ref):
    c = c_ref[...]
    ca = (c * jax.nn.sigmoid(c)).astype(BF16)
    o_ref[0] = jnp.dot(ca, w_ref[0].astype(BF16), preferred_element_type=F32) + b_ref[0]


def _modulation(c, w_ada, b_ada):
    nb = 1536
    return pl.pallas_call(
        _mod_kernel,
        grid=(DEPTH, N_MOD * D_MODEL // nb),
        in_specs=[
            pl.BlockSpec((BATCH, D_MODEL), lambda l, j: (0, 0)),
            pl.BlockSpec((1, D_MODEL, nb), lambda l, j: (l, 0, j)),
            pl.BlockSpec((1, 1, nb), lambda l, j: (l, 0, j)),
        ],
        out_specs=pl.BlockSpec((1, BATCH, nb), lambda l, j: (l, 0, j)),
        out_shape=jax.ShapeDtypeStruct((DEPTH, BATCH, N_MOD * D_MODEL), F32),
        compiler_params=pltpu.CompilerParams(
            dimension_semantics=("arbitrary", "arbitrary"), vmem_limit_bytes=VMEM_LIMIT),
        name="adaln_mod",
    )(c, w_ada, b_ada.reshape(DEPTH, 1, N_MOD * D_MODEL))


TT_MIX = SSM_SUB * TT_SSM


def _mix_kernel(x_ref, mod_ref, pg_ref, win_ref, wb_ref, are_ref, aim_ref, wc_ref, d_ref, wglu_ref,
                bglu_ref, g_ref, q_ref, kv_ref, o_ref,
                hn_ref, winb_ref, uall_ref, utb_ref, otb_ref, state_ref, wglub_ref, *hs_refs):
    @pl.when(pl.program_id(0) == 0)
    def _():
        state_ref[...] = jnp.zeros_like(state_ref)
        wglub_ref[...] = wglu_ref[0].astype(BF16)
        winb_ref[...] = win_ref[0].astype(BF16)

    n_slab = SSM_WIDTH // LANES
    tiles_per_half = SCAN_COLS // MXU_DIM
    per = SCAN_CHUNKS // (SSM_WIDTH // MXU_DIM)
    u_col = ATTN_WIDTH + 2 * KV_WIDTH

    for b in range(BATCH):
        mod = mod_ref[0, b]
        hn_ref[b * TT_MIX:(b + 1) * TT_MIX, :] = (
            _rms(x_ref[b], pg_ref[0] * (1.0 + mod[1:2])) + mod[0:1]).astype(BF16)

    uall_ref[...] = jnp.dot(hn_ref[...], winb_ref[:, u_col:], preferred_element_type=F32)
    for sub in range(SSM_SUB):
        for b in range(BATCH):
            r0 = b * TT_MIX + sub * TT_SSM
            for s in range(n_slab):
                utb_ref[sub * n_slab + s, pl.ds(b, TT_SSM, stride=BATCH), :] = (
                    uall_ref[r0:r0 + TT_SSM, s * LANES:(s + 1) * LANES])

    def qkv_piece(c):
        if c == 0:
            q = jnp.dot(hn_ref[...], winb_ref[:, :ATTN_WIDTH], preferred_element_type=F32)
            q_ref[...] = (q * (HEAD_DIM ** -0.5)).astype(BF16).reshape(BATCH, TT_MIX, ATTN_WIDTH)
        else:
            kv = jnp.dot(hn_ref[...], winb_ref[:, ATTN_WIDTH:u_col], preferred_element_type=F32)
            k = kv[:, :KV_WIDTH]
            v = kv[:, KV_WIDTH:]
            shape = (BATCH, TT_MIX, KV_WIDTH)
            kv_ref[:, :, 0 * KV_WIDTH:1 * KV_WIDTH] = k.astype(BF16).reshape(shape)
            kv_ref[:, :, 1 * KV_WIDTH:2 * KV_WIDTH] = v.astype(BF16).reshape(shape)
            kv_ref[:, :, 2 * KV_WIDTH:3 * KV_WIDTH] = (
                pltpu.roll(k, HEAD_DIM, 1).astype(BF16).reshape(shape))
            kv_ref[:, :, 3 * KV_WIDTH:4 * KV_WIDTH] = (
                pltpu.roll(v, HEAD_DIM, 1).astype(BF16).reshape(shape))


    def project_in(sub, k):
        ub = utb_ref[sub * n_slab + k]
        hs = hs_refs[sub * SCAN_CHUNKS + k]
        for part in range(2):
            for j in range(tiles_per_half):
                n = part * (STATE_COLS // MXU_DIM) + k * tiles_per_half + j
                c0 = part * SCAN_COLS + j * MXU_DIM
                hs[:, c0:c0 + MXU_DIM] = _dot_f32_lhs(ub, wb_ref[0, n])

    def scan(sub, k):
        hs = hs_refs[sub * SCAN_CHUNKS + k]
        cols = slice(k * SCAN_COLS, (k + 1) * SCAN_COLS)
        ar = jnp.broadcast_to(are_ref[0, :, cols], (BATCH, SCAN_COLS))
        ai = jnp.broadcast_to(aim_ref[0, :, cols], (BATCH, SCAN_COLS))
        hr = state_ref[k, :, :SCAN_COLS]
        hi = state_ref[k, :, SCAN_COLS:]
        for t in range(TT_SSM):
            rows = slice(t * BATCH, (t + 1) * BATCH)
            nr = ar * hr - ai * hi + hs[rows, :SCAN_COLS]
            ni = ar * hi + ai * hr + hs[rows, SCAN_COLS:]
            hs[rows, :SCAN_COLS] = nr
            hs[rows, SCAN_COLS:] = ni
            hr, hi = nr, ni
        state_ref[k, :, :SCAN_COLS] = hr
        state_ref[k, :, SCAN_COLS:] = hi

    def project_out(sub, n):
        chunks = range(n * per, (n + 1) * per)
        h = jnp.concatenate([hs_refs[sub * SCAN_CHUNKS + k][...] for k in chunks], axis=1)
        w = wc_ref[0, n * per:(n + 1) * per].reshape(per * 2 * SCAN_COLS, MXU_DIM)
        return _dot_f32_lhs(h, w)

    def finish(sub, ys):
        u = jnp.concatenate([utb_ref[sub * n_slab + s] for s in range(n_slab)], axis=1)
        y = jnp.concatenate(ys, axis=1) + d_ref[0] * u
        z = jax.nn.gelu(y)
        gate = jax.nn.sigmoid(
            jnp.dot(z.astype(BF16), wglub_ref[...], preferred_element_type=F32) + bglu_ref[0])
        res = _rms(z * gate, g_ref[0])
        for s in range(n_slab):
            otb_ref[sub * n_slab + s] = res[:, s * LANES:(s + 1) * LANES]
        t0 = sub * TT_SSM
        for b in range(BATCH):
            for s in range(n_slab):
                c0 = b * SSM_WIDTH + s * LANES
                o_ref[t0:t0 + TT_SSM, c0:c0 + LANES] = (
                    otb_ref[sub * n_slab + s, pl.ds(b, TT_SSM, stride=BATCH), :].astype(BF16))

    pending = None
    qkv = [0, 1]
    for sub in range(SSM_SUB):
        project_in(sub, 0)
        for k in range(SCAN_CHUNKS):
            if k + 1 < SCAN_CHUNKS:
                project_in(sub, k + 1)
            if pending is not None and k < len(pending[1]):
                pending[2].append(project_out(pending[0], pending[1][k]))
            elif qkv and k % 2 == 1:
                qkv_piece(qkv.pop(0))
            scan(sub, k)
        if pending is not None:
            finish(pending[0], pending[2])
        pending = (sub, list(range(SSM_WIDTH // MXU_DIM)), [])
    finish(pending[0], [project_out(pending[0], n) for n in pending[1]])
    for c in qkv:
        qkv_piece(c)


def _mix(layer, x, mod, pre_g, w_in, wb, a_re, a_im, wc, d_skip, w_glu, b_glu, ssm_g):
    rows = TT_SSM * BATCH
    vec = lambda n: pl.BlockSpec((1, 1, n), lambda i: (layer, 0, 0))
    return pl.pallas_call(
        _mix_kernel,
        grid=(SEQ // TT_MIX,),
        in_specs=[
            pl.BlockSpec((BATCH, TT_MIX, D_MODEL), lambda i: (0, i, 0)),
            pl.BlockSpec((1, BATCH, N_MOD, D_MODEL), lambda i: (layer, 0, 0, 0)),
            vec(D_MODEL),
            pl.BlockSpec((1, D_MODEL, IN_WIDTH), lambda i: (layer, 0, 0)),
            pl.BlockSpec((1,) + wb.shape[1:], lambda i: (layer, 0, 0, 0)),
            vec(STATE_COLS), vec(STATE_COLS),
            pl.BlockSpec((1,) + wc.shape[1:], lambda i: (layer, 0, 0, 0)),
            vec(SSM_WIDTH),
            pl.BlockSpec((1, SSM_WIDTH, SSM_WIDTH), lambda i: (layer, 0, 0)),
            vec(SSM_WIDTH), vec(SSM_WIDTH),
        ],
        out_specs=[
            pl.BlockSpec((BATCH, TT_MIX, ATTN_WIDTH), lambda i: (0, i, 0)),
            pl.BlockSpec((BATCH, TT_MIX, 4 * KV_WIDTH), lambda i: (0, i, 0)),
            pl.BlockSpec((TT_MIX, BATCH * SSM_WIDTH), lambda i: (i, 0)),
        ],
        out_shape=[
            jax.ShapeDtypeStruct((BATCH, SEQ, ATTN_WIDTH), BF16),
            jax.ShapeDtypeStruct((BATCH, SEQ, 4 * KV_WIDTH), BF16),
            jax.ShapeDtypeStruct((SEQ, BATCH * SSM_WIDTH), BF16),
        ],
        scratch_shapes=[
            pltpu.VMEM((BATCH * TT_MIX, D_MODEL), BF16),
            pltpu.VMEM((D_MODEL, IN_WIDTH), BF16),
            pltpu.VMEM((BATCH * TT_MIX, SSM_WIDTH), F32),
            pltpu.VMEM((SSM_SUB * SSM_WIDTH // LANES, rows, LANES), F32),
            pltpu.VMEM((SSM_SUB * SSM_WIDTH // LANES, rows, LANES), F32),
            pltpu.VMEM((SCAN_CHUNKS, BATCH, 2 * SCAN_COLS), F32),
            pltpu.VMEM((SSM_WIDTH, SSM_WIDTH), BF16),
        ] + [pltpu.VMEM((rows, 2 * SCAN_COLS), F32) for _ in range(SSM_SUB * SCAN_CHUNKS)],
        compiler_params=pltpu.CompilerParams(
            dimension_semantics=("arbitrary",), vmem_limit_bytes=VMEM_LIMIT),
        name="mix_in_s5",
    )(x, mod, pre_g, w_in, wb, a_re, a_im, wc, d_skip, w_glu, b_glu, ssm_g)


def _attn_kernel(q_ref, kvc_ref, kvp_ref, ssm_ref, x_ref, mod_ref, sink_ref, ag_ref, wout_ref,
                 pg_ref, o_ref, kpad_ref, vpad_ref, bias_ref, heads_ref, st_ref, p_ref, attn_ref,
                 mixed_ref, woutb_ref):
    i = pl.program_id(1)
    nq = TQ_ATT // WINDOW

    @pl.when((pl.program_id(0) == 0) & (i == 0))
    def _():
        j = lax.broadcasted_iota(jnp.int32, (2 * WINDOW, WINDOW), 0)
        r = lax.broadcasted_iota(jnp.int32, (2 * WINDOW, WINDOW), 1)
        diff = WINDOW + r - j
        valid = (diff >= 0) & (diff < WINDOW)
        for h in range(N_Q_HEADS):
            slope = 2.0 ** (-8.0 * (h + 1) / N_Q_HEADS)
            bias = -slope * diff.astype(F32)
            bias_ref[h] = jnp.where(valid, bias, NEG_INF)
            bias_ref[N_Q_HEADS + h] = jnp.where(valid & (j >= WINDOW), bias, NEG_INF)
        woutb_ref[...] = wout_ref[0].astype(BF16)

    lane = lax.broadcasted_iota(jnp.int32, (1, LANES), 1)
    for src, rows in ((kvp_ref, slice(0, WINDOW)), (kvc_ref, slice(WINDOW, WINDOW + TQ_ATT))):
        for hk in range(N_KV_HEADS):
            for par in range(2):
                keep = (lane >= par * HEAD_DIM) & (lane < (par + 1) * HEAD_DIM)
                off = 0 if par == hk else 2 * KV_WIDTH
                kpad_ref[hk * 2 + par, rows, :] = jnp.where(
                    keep, src[0, :, off:off + KV_WIDTH], jnp.zeros((), BF16))
                vpad_ref[hk * 2 + par, rows, :] = jnp.where(
                    keep, src[0, :, off + KV_WIDTH:off + 2 * KV_WIDTH], jnp.zeros((), BF16))

    nt = (((1,), (1,)), ((), ()))
    tn = (((0,), (0,)), ((), ()))

    def scores(jb):
        q0 = jb * WINDOW
        for hk in range(N_KV_HEADS):
            qq = jnp.concatenate(
                [q_ref[0, q0:q0 + WINDOW, (2 * hk + l2) * LANES:(2 * hk + l2 + 1) * LANES]
                 for l2 in range(2)], axis=0)
            for par in range(2):
                kp = kpad_ref[hk * 2 + par, q0:q0 + 2 * WINDOW, :]
                st_ref[(jb % 2) * 2 * N_KV_HEADS + hk * 2 + par] = lax.dot_general(
                    kp, qq, nt, preferred_element_type=F32)

    def softmax(jb):
        table = jnp.where(i == 0, N_Q_HEADS, 0) if jb == 0 else 0
        for hk in range(N_KV_HEADS):
            for par in range(2):
                for l2 in range(2):
                    h = Q_PER_KV * hk + 2 * l2 + par
                    s = (st_ref[(jb % 2) * 2 * N_KV_HEADS + hk * 2 + par, :,
                                l2 * WINDOW:(l2 + 1) * WINDOW] + bias_ref[table + h])
                    sink = sink_ref[0, 0, h]
                    m = jnp.maximum(jnp.max(s, axis=0, keepdims=True), sink)
                    e = jnp.exp(s - m)
                    denom = jnp.sum(e, axis=0, keepdims=True) + jnp.exp(sink - m)
                    p_ref[(jb % 2) * N_Q_HEADS + h] = (e * (1.0 / denom)).astype(BF16)

    def values(jb):
        q0 = jb * WINDOW
        for lt in range(N_Q_HEADS // 2):
            hk = 2 * lt // Q_PER_KV
            acc = None
            for par in range(2):
                vp = vpad_ref[hk * 2 + par, q0:q0 + 2 * WINDOW, :]
                o = lax.dot_general(p_ref[(jb % 2) * N_Q_HEADS + 2 * lt + par], vp, tn,
                                    preferred_element_type=F32)
                acc = o if acc is None else acc + o
            heads_ref[q0:q0 + WINDOW, lt * LANES:(lt + 1) * LANES] = acc

    mod = mod_ref[0, 0]
    gate = mod[2:3] * pg_ref[0]

    def out_proj_pieces(part):
        rows = slice(part * (TQ_ATT // ATT_SUB), (part + 1) * (TQ_ATT // ATT_SUB))
        n_col = D_MODEL // MXU_DIM

        def piece(c):
            if c == 0:
                attn_ref[rows, :] = _rms(heads_ref[rows, :], ag_ref[0]).astype(BF16)
            cols = slice(c * MXU_DIM, (c + 1) * MXU_DIM)
            heads = jnp.concatenate([attn_ref[rows, :], ssm_ref[rows, :]], axis=1)
            mixed_ref[rows, cols] = jnp.dot(heads, woutb_ref[:, cols], preferred_element_type=F32)
            if c == n_col - 1:
                o_ref[0, rows, :] = x_ref[0, rows, :] + _rms(mixed_ref[rows, :], gate)

        return [functools.partial(piece, c) for c in range(n_col)]

    per_part = nq // ATT_SUB
    queue = []
    scores(0)
    for jb in range(nq):
        if jb + 1 < nq:
            scores(jb + 1)
        softmax(jb)
        values(jb)
        for _ in range(-(-len(queue) // (per_part - jb % per_part))):
            queue.pop(0)()
        if (jb + 1) % per_part == 0:
            queue.extend(out_proj_pieces(jb // per_part))
    for piece in queue:
        piece()


def _attn(layer, q, kv, ssm_tm, x, mod, sinks, attn_g, w_out, post_g):
    nt = SEQ // TQ_ATT
    per_tile = TQ_ATT // WINDOW
    vec = lambda n: pl.BlockSpec((1, 1, n), lambda b, i: (layer, 0, 0))
    return pl.pallas_call(
        _attn_kernel,
        grid=(BATCH, nt),
        in_specs=[
            pl.BlockSpec((1, TQ_ATT, ATTN_WIDTH), lambda b, i: (b, i, 0)),
            pl.BlockSpec((1, TQ_ATT, 4 * KV_WIDTH), lambda b, i: (b, i, 0)),
            pl.BlockSpec((1, WINDOW, 4 * KV_WIDTH),
                         lambda b, i: (b, jnp.maximum(i * per_tile - 1, 0), 0)),
            pl.BlockSpec((TQ_ATT, SSM_WIDTH), lambda b, i: (i, b)),
            pl.BlockSpec((1, TQ_ATT, D_MODEL), lambda b, i: (b, i, 0)),
            pl.BlockSpec((1, 1, N_MOD, D_MODEL), lambda b, i: (layer, b, 0, 0)),
            pl.BlockSpec((1, 1, N_Q_HEADS), lambda b, i: (layer, 0, 0), memory_space=pltpu.SMEM),
            vec(ATTN_WIDTH),
            pl.BlockSpec((1, D_MODEL, D_MODEL), lambda b, i: (layer, 0, 0)),
            vec(D_MODEL),
        ],
        out_specs=pl.BlockSpec((1, TQ_ATT, D_MODEL), lambda b, i: (b, i, 0)),
        out_shape=jax.ShapeDtypeStruct((BATCH, SEQ, D_MODEL), F32),
        scratch_shapes=[
            pltpu.VMEM((2 * N_KV_HEADS, WINDOW + TQ_ATT, LANES), BF16),
            pltpu.VMEM((2 * N_KV_HEADS, WINDOW + TQ_ATT, LANES), BF16),
            pltpu.VMEM((2 * N_Q_HEADS, 2 * WINDOW, WINDOW), F32),
            pltpu.VMEM((TQ_ATT, ATTN_WIDTH), F32),
            pltpu.VMEM((2 * 2 * N_KV_HEADS, 2 * WINDOW, 2 * WINDOW), F32),
            pltpu.VMEM((2 * N_Q_HEADS, 2 * WINDOW, WINDOW), BF16),
            pltpu.VMEM((TQ_ATT, ATTN_WIDTH), BF16),
            pltpu.VMEM((TQ_ATT, D_MODEL), F32),
            pltpu.VMEM((D_MODEL, D_MODEL), BF16),
        ],
        compiler_params=pltpu.CompilerParams(
            dimension_semantics=("arbitrary", "arbitrary"), vmem_limit_bytes=VMEM_LIMIT),
        name="attn_outproj",
    )(q, kv, kv, ssm_tm, x, mod, sinks, attn_g, w_out, post_g)


def _mlp_kernel(x_ref, mod_ref, g_ref, w1_ref, w2_ref, pg_ref, o_ref, acc_ref):
    x = x_ref[0]
    mod = mod_ref[0, 0]
    h = (_rms(x, g_ref[0] * (1.0 + mod[4:5])) + mod[3:4]).astype(BF16)
    for c in range(D_FF // FF_CHUNK):
        cols = slice(c * FF_CHUNK, (c + 1) * FF_CHUNK)
        f = jnp.dot(h, w1_ref[0, :, cols], preferred_element_type=F32)
        f = jnp.square(jnp.maximum(f, 0.0)).astype(BF16)
        part = jnp.dot(f, w2_ref[0, cols, :], preferred_element_type=F32)
        if c == 0:
            acc_ref[...] = part
        else:
            acc_ref[...] += part
    o_ref[0] = x + _rms(acc_ref[...], mod[5:6] * pg_ref[0])


def _mlp(layer, x, mod, pre_g, w1, w2, post_g):
    nt = SEQ // TM_MLP
    vec = lambda n: pl.BlockSpec((1, 1, n), lambda b, i: (layer, 0, 0))
    return pl.pallas_call(
        _mlp_kernel,
        grid=(BATCH, nt),
        in_specs=[
            pl.BlockSpec((1, TM_MLP, D_MODEL), lambda b, i: (b, i, 0)),
            pl.BlockSpec((1, 1, N_MOD, D_MODEL), lambda b, i: (layer, b, 0, 0)),
            vec(D_MODEL),
            pl.BlockSpec((1, D_MODEL, D_FF), lambda b, i: (layer, 0, 0), pipeline_mode=pl.Buffered(1)),
            pl.BlockSpec((1, D_FF, D_MODEL), lambda b, i: (layer, 0, 0), pipeline_mode=pl.Buffered(1)),
            vec(D_MODEL),
        ],
        out_specs=pl.BlockSpec((1, TM_MLP, D_MODEL), lambda b, i: (b, i, 0)),
        out_shape=jax.ShapeDtypeStruct((BATCH, SEQ, D_MODEL), F32),
        scratch_shapes=[pltpu.VMEM((TM_MLP, D_MODEL), F32)],
        compiler_params=pltpu.CompilerParams(
            dimension_semantics=("arbitrary", "arbitrary"), vmem_limit_bytes=VMEM_LIMIT),
        name="mlp",
    )(x, mod, pre_g, w1, w2, post_g)


def _ssm_params(lam_re, lam_im, log_dt, b_re, b_im, c_re, c_im):
    dt = jnp.exp(log_dt)[..., None]
    mag = jnp.exp(lam_re * dt)
    ang = lam_im * dt
    ab_r = mag * jnp.cos(ang)
    ab_i = mag * jnp.sin(ang)
    nr = ab_r - 1.0
    ni = ab_i
    den = lam_re * lam_re + lam_im * lam_im
    f_r = (nr * lam_re + ni * lam_im) / den
    f_i = (ni * lam_re - nr * lam_im) / den
    bb_r = f_r[..., None] * b_re - f_i[..., None] * b_im
    bb_i = f_r[..., None] * b_im + f_i[..., None] * b_re
    gpt = MXU_DIM // STATE
    gps = LANES // SSM_GROUP
    n_t = STATE_COLS // MXU_DIM
    n_i, g_i, h_i = np.meshgrid(np.arange(n_t), np.arange(gps), np.arange(gpt), indexing='ij')
    sel_b = jnp.asarray(g_i == gpt * (n_i % (gps // gpt)) + h_i, F32)

    def b_tiles(bb):
        bb = bb.reshape(DEPTH, n_t, gpt, STATE, SSM_GROUP)
        w = jnp.einsum('lnhpc,ngh->lngchp', bb, sel_b)
        return w.reshape(DEPTH, n_t, LANES, MXU_DIM)

    wb = jnp.concatenate([b_tiles(bb_r), b_tiles(bb_i)], axis=1).astype(BF16)

    gpc = MXU_DIM // SSM_GROUP
    eye = jnp.eye(gpc, dtype=F32)

    def c_tiles(cc):
        cc = cc.reshape(DEPTH, N_SSM_GROUPS // gpc, gpc, SSM_GROUP, STATE)
        w = jnp.einsum('lngcp,gh->lngphc', cc, eye)
        return w.reshape(DEPTH, N_SSM_GROUPS // gpc, gpc * STATE, MXU_DIM).astype(BF16)

    per = SCAN_CHUNKS // (SSM_WIDTH // MXU_DIM)
    split = lambda w: w.reshape(DEPTH, SSM_WIDTH // MXU_DIM, per, SCAN_COLS, MXU_DIM)
    wc = jnp.concatenate([split(c_tiles(c_re)), split(c_tiles(-c_im))], axis=3)
    wc = wc.reshape(DEPTH, SCAN_CHUNKS, 2 * SCAN_COLS, MXU_DIM)
    a_re = ab_r.reshape(DEPTH, 1, STATE_COLS)
    a_im = ab_i.reshape(DEPTH, 1, STATE_COLS)
    return wb, a_re, a_im, wc


def kernel(x, c, w_ada, b_ada, pre_mix_g, w_in, attn_sinks, lam_re, lam_im, log_dt, b_re, b_im,
           c_re, c_im, d_skip, w_glu, b_glu, attn_out_g, ssm_out_g, w_out, post_mix_g, pre_mlp_g,
           w_mlp_in, w_mlp_out, post_mlp_g):
    row = lambda a: a.reshape(DEPTH, 1, a.shape[-1])
    mod = _modulation(c, w_ada, b_ada).reshape(DEPTH, BATCH, N_MOD, D_MODEL)
    wb, a_re, a_im, wc = _ssm_params(lam_re, lam_im, log_dt, b_re, b_im, c_re, c_im)
    w1_b = w_mlp_in.astype(BF16)
    w2_b = w_mlp_out.astype(BF16)
    sinks = row(attn_sinks)
    for layer in range(DEPTH):
        q, kv, ssm_tm = _mix(layer, x, mod, row(pre_mix_g), w_in, wb, a_re, a_im, wc,
                             row(d_skip), w_glu, row(b_glu), row(ssm_out_g))
        x = _attn(layer, q, kv, ssm_tm, x, mod, sinks,
                  row(attn_out_g), w_out, row(post_mix_g))
        x = _mlp(layer, x, mod, row(pre_mlp_g), w1_b, w2_b, row(post_mlp_g))
    return x
```

```python
import functools

import jax
import jax.numpy as jnp
from jax import lax
from jax.experimental import pallas as pl
from jax.experimental.pallas import tpu as pltpu

D_MODEL = 1024
BATCH = 8
SEQ = 4096
DEPTH = 4
ATTN_WIDTH = 512
SSM_WIDTH = 512
HEAD_DIM = 64
N_Q_HEADS = 8
N_KV_HEADS = 2
Q_PER_KV = 4
KV_WIDTH = 128
WINDOW = 128
SSM_GROUP = 16
N_SSM_GROUPS = 32
STATE = 64
D_FF = 4096
IN_WIDTH = 1280
N_MOD = 6
EPS = 1e-6
NEG_INF = -1e30

LANES = 128
SUBLANES = 8
MXU_DIM = 256

STATE_COLS = N_SSM_GROUPS * STATE
BF16 = jnp.bfloat16
F32 = jnp.float32

VMEM_LIMIT = 56 * 1024 * 1024

TT_SSM = 64
SSM_SUB = 2
TQ_ATT = 1024
ATT_SUB = 2
TM_MLP = 1024
FF_CHUNK = 1024
SCAN_COLS = LANES * STATE // SSM_GROUP
SCAN_CHUNKS = STATE_COLS // SCAN_COLS


def _rms(x, g):
    return x * lax.rsqrt(jnp.mean(x * x, axis=-1, keepdims=True) + EPS) * g


def _dot_f32_lhs(a, w):
    return lax.dot_general(a, w, (((1,), (0,)), ((), ())), preferred_element_type=F32)


def _mod_kernel(c_ref, w_ref, b_ref, o_ref):
    c = c_ref[...]
    ca = (c * jax.nn.sigmoid(c)).astype(BF16)
    o_ref[0] = jnp.dot(ca, w_ref[0].astype(BF16), preferred_element_type=F32) + b_ref[0]


def _modulation(c, w_ada, b_ada):
    nb = 1536
    return pl.pallas_call(
        _mod_kernel,
        grid=(DEPTH, N_MOD * D_MODEL // nb),
        in_specs=[
            pl.BlockSpec((BATCH, D_MODEL), lambda l, j: (0, 0)),
            pl.BlockSpec((1, D_MODEL, nb), lambda l, j: (l, 0, j)),
            pl.BlockSpec((1, 1, nb), lambda l, j: (l, 0, j)),
        ],
        out_specs=pl.BlockSpec((1, BATCH, nb), lambda l, j: (l, 0, j)),
        out_shape=jax.ShapeDtypeStruct((DEPTH, BATCH, N_MOD * D_MODEL), F32),
        compiler_params=pltpu.CompilerParams(
            dimension_semantics=("arbitrary", "arbitrary"), vmem_limit_bytes=VMEM_LIMIT),
        name="adaln_mod",
    )(c, w_ada, b_ada.reshape(DEPTH, 1, N_MOD * D_MODEL))


TT_MIX = SSM_SUB * TT_SSM


def _mix_kernel(x_ref, mod_ref, pg_ref, win_ref, wb_ref, are_ref, aim_ref, wc_ref, d_ref, wglu_ref,
                bglu_ref, g_ref, q_ref, kv_ref, o_ref,
                hn_ref, winb_ref, uall_ref, utb_ref, otb_ref, state_ref, wglub_ref, *hs_refs):
    @pl.when(pl.program_id(0) == 0)
    def _():
        state_ref[...] = jnp.zeros_like(state_ref)
        wglub_ref[...] = wglu_ref[0].astype(BF16)
        winb_ref[...] = win_ref[0].astype(BF16)

    n_slab = SSM_WIDTH // LANES
    tiles_per_half = SCAN_COLS // MXU_DIM
    per = SCAN_CHUNKS // (SSM_WIDTH // MXU_DIM)
    u_col = ATTN_WIDTH + 2 * KV_WIDTH

    half = BATCH // 2
    for part in range(2):
        for b in range(part * half, (part + 1) * half):
            mod = mod_ref[0, b]
            hn_ref[b * TT_MIX:(b + 1) * TT_MIX, :] = (
                _rms(x_ref[b], pg_ref[0] * (1.0 + mod[1:2])) + mod[0:1]).astype(BF16)
        rows = slice(part * half * TT_MIX, (part + 1) * half * TT_MIX)
        uall_ref[rows, :] = jnp.dot(hn_ref[rows, :], winb_ref[:, u_col:], preferred_element_type=F32)

    for sub in range(SSM_SUB):
        for b in range(BATCH):
            r0 = b * TT_MIX + sub * TT_SSM
            for s in range(n_slab):
                utb_ref[sub * n_slab + s, pl.ds(b, TT_SSM, stride=BATCH), :] = (
                    uall_ref[r0:r0 + TT_SSM, s * LANES:(s + 1) * LANES])

    def qkv_piece(c):
        if c == 0:
            q = jnp.dot(hn_ref[...], winb_ref[:, :ATTN_WIDTH], preferred_element_type=F32)
            q_ref[...] = (q * (HEAD_DIM ** -0.5)).astype(BF16).reshape(BATCH, TT_MIX, ATTN_WIDTH)
        else:
            kv = jnp.dot(hn_ref[...], winb_ref[:, ATTN_WIDTH:u_col], preferred_element_type=F32)
            k = kv[:, :KV_WIDTH]
            v = kv[:, KV_WIDTH:]
            shape = (BATCH, TT_MIX, KV_WIDTH)
            kv_ref[:, :, 0 * KV_WIDTH:1 * KV_WIDTH] = k.astype(BF16).reshape(shape)
            kv_ref[:, :, 1 * KV_WIDTH:2 * KV_WIDTH] = v.astype(BF16).reshape(shape)
            kv_ref[:, :, 2 * KV_WIDTH:3 * KV_WIDTH] = (
                pltpu.roll(k, HEAD_DIM, 1).astype(BF16).reshape(shape))
            kv_ref[:, :, 3 * KV_WIDTH:4 * KV_WIDTH] = (
                pltpu.roll(v, HEAD_DIM, 1).astype(BF16).reshape(shape))


    def project_in(sub, k):
        ub = utb_ref[sub * n_slab + k]
        hs = hs_refs[sub * SCAN_CHUNKS + k]
        for part in range(2):
            for j in range(tiles_per_half):
                n = part * (STATE_COLS // MXU_DIM) + k * tiles_per_half + j
                c0 = part * SCAN_COLS + j * MXU_DIM
                hs[:, c0:c0 + MXU_DIM] = _dot_f32_lhs(ub, wb_ref[0, n])

    def scan(sub, k):
        hs = hs_refs[sub * SCAN_CHUNKS + k]
        cols = slice(k * SCAN_COLS, (k + 1) * SCAN_COLS)
        ar = jnp.broadcast_to(are_ref[0, :, cols], (BATCH, SCAN_COLS))
        ai = jnp.broadcast_to(aim_ref[0, :, cols], (BATCH, SCAN_COLS))
        hr = state_ref[k, :, :SCAN_COLS]
        hi = state_ref[k, :, SCAN_COLS:]
        for t in range(TT_SSM):
            rows = slice(t * BATCH, (t + 1) * BATCH)
            nr = ar * hr - ai * hi + hs[rows, :SCAN_COLS]
            ni = ar * hi + ai * hr + hs[rows, SCAN_COLS:]
            hs[rows, :SCAN_COLS] = nr
            hs[rows, SCAN_COLS:] = ni
            hr, hi = nr, ni
        state_ref[k, :, :SCAN_COLS] = hr
        state_ref[k, :, SCAN_COLS:] = hi

    def project_out(sub, n):
        chunks = range(n * per, (n + 1) * per)
        h = jnp.concatenate([hs_refs[sub * SCAN_CHUNKS + k][...] for k in chunks], axis=1)
        w = wc_ref[0, n * per:(n + 1) * per].reshape(per * 2 * SCAN_COLS, MXU_DIM)
        return _dot_f32_lhs(h, w)

    def finish(sub, ys):
        u = jnp.concatenate([utb_ref[sub * n_slab + s] for s in range(n_slab)], axis=1)
        y = jnp.concatenate(ys, axis=1) + d_ref[0] * u
        z = jax.nn.gelu(y)
        gate = jax.nn.sigmoid(
            jnp.dot(z.astype(BF16), wglub_ref[...], preferred_element_type=F32) + bglu_ref[0])
        res = _rms(z * gate, g_ref[0])
        for s in range(n_slab):
            otb_ref[sub * n_slab + s] = res[:, s * LANES:(s + 1) * LANES]
        t0 = sub * TT_SSM
        for b in range(BATCH):
            for s in range(n_slab):
                c0 = b * SSM_WIDTH + s * LANES
                o_ref[t0:t0 + TT_SSM, c0:c0 + LANES] = (
                    otb_ref[sub * n_slab + s, pl.ds(b, TT_SSM, stride=BATCH), :].astype(BF16))

    pending = None
    qkv = [0, 1]
    for sub in range(SSM_SUB):
        project_in(sub, 0)
        for k in range(SCAN_CHUNKS):
            if k + 1 < SCAN_CHUNKS:
                project_in(sub, k + 1)
            if pending is not None and k < len(pending[1]):
                pending[2].append(project_out(pending[0], pending[1][k]))
            elif qkv and k % 2 == 1:
                qkv_piece(qkv.pop(0))
            scan(sub, k)
        if pending is not None:
            finish(pending[0], pending[2])
        pending = (sub, list(range(SSM_WIDTH // MXU_DIM)), [])
    finish(pending[0], [project_out(pending[0], n) for n in pending[1]])
    for c in qkv:
        qkv_piece(c)


def _mix(layer, x, mod, pre_g, w_in, wb, a_re, a_im, wc, d_skip, w_glu, b_glu, ssm_g):
    rows = TT_SSM * BATCH
    vec = lambda n: pl.BlockSpec((1, 1, n), lambda i: (layer, 0, 0))
    return pl.pallas_call(
        _mix_kernel,
        grid=(SEQ // TT_MIX,),
        in_specs=[
            pl.BlockSpec((BATCH, TT_MIX, D_MODEL), lambda i: (0, i, 0)),
            pl.BlockSpec((1, BATCH, N_MOD, D_MODEL), lambda i: (layer, 0, 0, 0)),
            vec(D_MODEL),
            pl.BlockSpec((1, D_MODEL, IN_WIDTH), lambda i: (layer, 0, 0)),
            pl.BlockSpec((1,) + wb.shape[1:], lambda i: (layer, 0, 0, 0)),
            vec(STATE_COLS), vec(STATE_COLS),
            pl.BlockSpec((1,) + wc.shape[1:], lambda i: (layer, 0, 0, 0)),
            vec(SSM_WIDTH),
            pl.BlockSpec((1, SSM_WIDTH, SSM_WIDTH), lambda i: (layer, 0, 0)),
            vec(SSM_WIDTH), vec(SSM_WIDTH),
        ],
        out_specs=[
            pl.BlockSpec((BATCH, TT_MIX, ATTN_WIDTH), lambda i: (0, i, 0)),
            pl.BlockSpec((BATCH, TT_MIX, 4 * KV_WIDTH), lambda i: (0, i, 0)),
            pl.BlockSpec((TT_MIX, BATCH * SSM_WIDTH), lambda i: (i, 0)),
        ],
        out_shape=[
            jax.ShapeDtypeStruct((BATCH, SEQ, ATTN_WIDTH), BF16),
            jax.ShapeDtypeStruct((BATCH, SEQ, 4 * KV_WIDTH), BF16),
            jax.ShapeDtypeStruct((SEQ, BATCH * SSM_WIDTH), BF16),
        ],
        scratch_shapes=[
            pltpu.VMEM((BATCH * TT_MIX, D_MODEL), BF16),
            pltpu.VMEM((D_MODEL, IN_WIDTH), BF16),
            pltpu.VMEM((BATCH * TT_MIX, SSM_WIDTH), F32),
            pltpu.VMEM((SSM_SUB * SSM_WIDTH // LANES, rows, LANES), F32),
            pltpu.VMEM((SSM_SUB * SSM_WIDTH // LANES, rows, LANES), F32),
            pltpu.VMEM((SCAN_CHUNKS, BATCH, 2 * SCAN_COLS), F32),
            pltpu.VMEM((SSM_WIDTH, SSM_WIDTH), BF16),
        ] + [pltpu.VMEM((rows, 2 * SCAN_COLS), F32) for _ in range(SSM_SUB * SCAN_CHUNKS)],
        compiler_params=pltpu.CompilerParams(
            dimension_semantics=("arbitrary",), vmem_limit_bytes=VMEM_LIMIT),
        name="mix_in_s5",
    )(x, mod, pre_g, w_in, wb, a_re, a_im, wc, d_skip, w_glu, b_glu, ssm_g)


def _attn_kernel(q_ref, kvc_ref, kvp_ref, ssm_ref, x_ref, mod_ref, sink_ref, ag_ref, wout_ref,
                 pg_ref, o_ref, kpad_ref, vpad_ref, bias_ref, heads_ref, st_ref, p_ref, attn_ref,
                 mixed_ref, woutb_ref):
    i = pl.program_id(1)
    nq = TQ_ATT // WINDOW

    @pl.when((pl.program_id(0) == 0) & (i == 0))
    def _():
        j = lax.broadcasted_iota(jnp.int32, (2 * WINDOW, WINDOW), 0)
        r = lax.broadcasted_iota(jnp.int32, (2 * WINDOW, WINDOW), 1)
        diff = WINDOW + r - j
        valid = (diff >= 0) & (diff < WINDOW)
        for h in range(N_Q_HEADS):
            slope = 2.0 ** (-8.0 * (h + 1) / N_Q_HEADS)
            bias = -slope * diff.astype(F32)
            bias_ref[h] = jnp.where(valid, bias, NEG_INF)
            bias_ref[N_Q_HEADS + h] = jnp.where(valid & (j >= WINDOW), bias, NEG_INF)
        woutb_ref[...] = wout_ref[0].astype(BF16)

    lane = lax.broadcasted_iota(jnp.int32, (1, LANES), 1)
    for src, rows in ((kvp_ref, slice(0, WINDOW)), (kvc_ref, slice(WINDOW, WINDOW + TQ_ATT))):
        for hk in range(N_KV_HEADS):
            for par in range(2):
                keep = (lane >= par * HEAD_DIM) & (lane < (par + 1) * HEAD_DIM)
                off = 0 if par == hk else 2 * KV_WIDTH
                kpad_ref[hk * 2 + par, rows, :] = jnp.where(
                    keep, src[0, :, off:off + KV_WIDTH], jnp.zeros((), BF16))
                vpad_ref[hk * 2 + par, rows, :] = jnp.where(
                    keep, src[0, :, off + KV_WIDTH:off + 2 * KV_WIDTH], jnp.zeros((), BF16))

    nt = (((1,), (1,)), ((), ()))
    tn = (((0,), (0,)), ((), ()))

    def scores(jb):
        q0 = jb * WINDOW
        for hk in range(N_KV_HEADS):
            qq = jnp.concatenate(
                [q_ref[0, q0:q0 + WINDOW, (2 * hk + l2) * LANES:(2 * hk + l2 + 1) * LANES]
                 for l2 in range(2)], axis=0)
            for par in range(2):
                kp = kpad_ref[hk * 2 + par, q0:q0 + 2 * WINDOW, :]
                st_ref[(jb % 2) * 2 * N_KV_HEADS + hk * 2 + par] = lax.dot_general(
                    kp, qq, nt, preferred_element_type=F32)

    def softmax(jb):
        table = jnp.where(i == 0, N_Q_HEADS, 0) if jb == 0 else 0
        for hk in range(N_KV_HEADS):
            for par in range(2):
                for l2 in range(2):
                    h = Q_PER_KV * hk + 2 * l2 + par
                    s = (st_ref[(jb % 2) * 2 * N_KV_HEADS + hk * 2 + par, :,
                                l2 * WINDOW:(l2 + 1) * WINDOW] + bias_ref[table + h])
                    sink = sink_ref[0, 0, h]
                    m = jnp.maximum(jnp.max(s, axis=0, keepdims=True), sink)
                    e = jnp.exp(s - m)
                    denom = jnp.sum(e, axis=0, keepdims=True) + jnp.exp(sink - m)
                    p_ref[(jb % 2) * N_Q_HEADS + h] = (e * (1.0 / denom)).astype(BF16)

    def values(jb):
        q0 = jb * WINDOW
        for lt in range(N_Q_HEADS // 2):
            hk = 2 * lt // Q_PER_KV
            acc = None
            for par in range(2):
                vp = vpad_ref[hk * 2 + par, q0:q0 + 2 * WINDOW, :]
                o = lax.dot_general(p_ref[(jb % 2) * N_Q_HEADS + 2 * lt + par], vp, tn,
                                    preferred_element_type=F32)
                acc = o if acc is None else acc + o
            heads_ref[q0:q0 + WINDOW, lt * LANES:(lt + 1) * LANES] = acc

    mod = mod_ref[0, 0]
    gate = mod[2:3] * pg_ref[0]

    def out_proj_pieces(part):
        rows = slice(part * (TQ_ATT // ATT_SUB), (part + 1) * (TQ_ATT // ATT_SUB))
        n_col = D_MODEL // MXU_DIM

        def piece(c):
            if c == 0:
                attn_ref[rows, :] = _rms(heads_ref[rows, :], ag_ref[0]).astype(BF16)
            cols = slice(c * MXU_DIM, (c + 1) * MXU_DIM)
            heads = jnp.concatenate([attn_ref[rows, :], ssm_ref[rows, :]], axis=1)
            mixed_ref[rows, cols] = jnp.dot(heads, woutb_ref[:, cols], preferred_element_type=F32)
            if c == n_col - 1:
                o_ref[0, rows, :] = x_ref[0, rows, :] + _rms(mixed_ref[rows, :], gate)

        return [functools.partial(piece, c) for c in range(n_col)]

    per_part = nq // ATT_SUB
    queue = []
    scores(0)
    for jb in range(nq):
        if jb + 1 < nq:
            scores(jb + 1)
        softmax(jb)
        values(jb)
        for _ in range(-(-len(queue) // (per_part - jb % per_part))):
            queue.pop(0)()
        if (jb + 1) % per_part == 0:
            queue.extend(out_proj_pieces(jb // per_part))
    for piece in queue:
        piece()


def _attn(layer, q, kv, ssm_tm, x, mod, sinks, attn_g, w_out, post_g):
    nt = SEQ // TQ_ATT
    per_tile = TQ_ATT // WINDOW
    vec = lambda n: pl.BlockSpec((1, 1, n), lambda b, i: (layer, 0, 0))
    return pl.pallas_call(
        _attn_kernel,
        grid=(BATCH, nt),
        in_specs=[
            pl.BlockSpec((1, TQ_ATT, ATTN_WIDTH), lambda b, i: (b, i, 0)),
            pl.BlockSpec((1, TQ_ATT, 4 * KV_WIDTH), lambda b, i: (b, i, 0)),
            pl.BlockSpec((1, WINDOW, 4 * KV_WIDTH),
                         lambda b, i: (b, jnp.maximum(i * per_tile - 1, 0), 0)),
            pl.BlockSpec((TQ_ATT, SSM_WIDTH), lambda b, i: (i, b)),
            pl.BlockSpec((1, TQ_ATT, D_MODEL), lambda b, i: (b, i, 0)),
            pl.BlockSpec((1, 1, N_MOD, D_MODEL), lambda b, i: (layer, b, 0, 0)),
            pl.BlockSpec((1, 1, N_Q_HEADS), lambda b, i: (layer, 0, 0), memory_space=pltpu.SMEM),
            vec(ATTN_WIDTH),
            pl.BlockSpec((1, D_MODEL, D_MODEL), lambda b, i: (layer, 0, 0)),
            vec(D_MODEL),
        ],
        out_specs=pl.BlockSpec((1, TQ_ATT, D_MODEL), lambda b, i: (b, i, 0)),
        out_shape=jax.ShapeDtypeStruct((BATCH, SEQ, D_MODEL), F32),
        scratch_shapes=[
            pltpu.VMEM((2 * N_KV_HEADS, WINDOW + TQ_ATT, LANES), BF16),
            pltpu.VMEM((2 * N_KV_HEADS, WINDOW + TQ_ATT, LANES), BF16),
            pltpu.VMEM((2 * N_Q_HEADS, 2 * WINDOW, WINDOW), F32),
            pltpu.VMEM((TQ_ATT, ATTN_WIDTH), F32),
            pltpu.VMEM((2 * 2 * N_KV_HEADS, 2 * WINDOW, 2 * WINDOW), F32),
            pltpu.VMEM((2 * N_Q_HEADS, 2 * WINDOW, WINDOW), BF16),
            pltpu.VMEM((TQ_ATT, ATTN_WIDTH), BF16),
            pltpu.VMEM((TQ_ATT, D_MODEL), F32),
            pltpu.VMEM((D_MODEL, D_MODEL), BF16),
        ],
        compiler_params=pltpu.CompilerParams(
            dimension_semantics=("arbitrary", "arbitrary"), vmem_limit_bytes=VMEM_LIMIT),
        name="attn_outproj",
    )(q, kv, kv, ssm_tm, x, mod, sinks, attn_g, w_out, post_g)


def _mlp_kernel(x_ref, mod_ref, g_ref, w1_ref, w2_ref, pg_ref, o_ref, acc_ref):
    x = x_ref[0]
    mod = mod_ref[0, 0]
    h = (_rms(x, g_ref[0] * (1.0 + mod[4:5])) + mod[3:4]).astype(BF16)
    for c in range(D_FF // FF_CHUNK):
        cols = slice(c * FF_CHUNK, (c + 1) * FF_CHUNK)
        f = jnp.dot(h, w1_ref[0, :, cols], preferred_element_type=F32)
        f = jnp.square(jnp.maximum(f, 0.0)).astype(BF16)
        part = jnp.dot(f, w2_ref[0, cols, :], preferred_element_type=F32)
        if c == 0:
            acc_ref[...] = part
        else:
            acc_ref[...] += part
    o_ref[0] = x + _rms(acc_ref[...], mod[5:6] * pg_ref[0])


def _mlp(layer, x, mod, pre_g, w1, w2, post_g):
    nt = SEQ // TM_MLP
    vec = lambda n: pl.BlockSpec((1, 1, n), lambda b, i: (layer, 0, 0))
    return pl.pallas_call(
        _mlp_kernel,
        grid=(BATCH, nt),
        in_specs=[
            pl.BlockSpec((1, TM_MLP, D_MODEL), lambda b, i: (b, i, 0)),
            pl.BlockSpec((1, 1, N_MOD, D_MODEL), lambda b, i: (layer, b, 0, 0)),
            vec(D_MODEL),
            pl.BlockSpec((1, D_MODEL, D_FF), lambda b, i: (layer, 0, 0), pipeline_mode=pl.Buffered(1)),
            pl.BlockSpec((1, D_FF, D_MODEL), lambda b, i: (layer, 0, 0), pipeline_mode=pl.Buffered(1)),
            vec(D_MODEL),
        ],
        out_specs=pl.BlockSpec((1, TM_MLP, D_MODEL), lambda b, i: (b, i, 0)),
        out_shape=jax.ShapeDtypeStruct((BATCH, SEQ, D_MODEL), F32),
        scratch_shapes=[pltpu.VMEM((TM_MLP, D_MODEL), F32)],
        compiler_params=pltpu.CompilerParams(
            dimension_semantics=("arbitrary", "arbitrary"), vmem_limit_bytes=VMEM_LIMIT),
        name="mlp",
    )(x, mod, pre_g, w1, w2, post_g)


def _ssm_params(lam_re, lam_im, log_dt, b_re, b_im, c_re, c_im):
    dt = jnp.exp(log_dt)[..., None]
    mag = jnp.exp(lam_re * dt)
    ang = lam_im * dt
    ab_r = mag * jnp.cos(ang)
    ab_i = mag * jnp.sin(ang)
    nr = ab_r - 1.0
    ni = ab_i
    den = lam_re * lam_re + lam_im * lam_im
    f_r = (nr * lam_re + ni * lam_im) / den
    f_i = (ni * lam_re - nr * lam_im) / den
    bb_r = f_r[..., None] * b_re - f_i[..., None] * b_im
    bb_i = f_r[..., None] * b_im + f_i[..., None] * b_re
    gpt = MXU_DIM // STATE
    gps = LANES // SSM_GROUP
    n_t = STATE_COLS // MXU_DIM
    shape_b = (n_t, LANES, MXU_DIM)
    n_i = lax.broadcasted_iota(jnp.int32, shape_b, 0)
    g_i = lax.broadcasted_iota(jnp.int32, shape_b, 1) // SSM_GROUP
    h_i = lax.broadcasted_iota(jnp.int32, shape_b, 2) // STATE
    mask_b = g_i == gpt * (n_i % (gps // gpt)) + h_i

    def b_tiles(bb):
        blk = bb.reshape(DEPTH, n_t, gpt, STATE, SSM_GROUP).transpose(0, 1, 4, 2, 3)
        blk = blk.reshape(DEPTH, n_t, 1, SSM_GROUP, MXU_DIM)
        rep = jnp.broadcast_to(blk, (DEPTH, n_t, gps, SSM_GROUP, MXU_DIM))
        return jnp.where(mask_b, rep.reshape(DEPTH, n_t, LANES, MXU_DIM), 0.0)

    wb = jnp.concatenate([b_tiles(bb_r), b_tiles(bb_i)], axis=1).astype(BF16)

    gpc = MXU_DIM // SSM_GROUP
    shape_c = (SCAN_CHUNKS, SCAN_COLS, MXU_DIM)
    k_i = lax.broadcasted_iota(jnp.int32, shape_c, 0)
    r_i = lax.broadcasted_iota(jnp.int32, shape_c, 1) // STATE
    c_i = lax.broadcasted_iota(jnp.int32, shape_c, 2) // SSM_GROUP
    mask_c = c_i == (gps * k_i + r_i) % gpc

    def c_rows(cc):
        blk = cc.transpose(0, 1, 3, 2).reshape(DEPTH, SCAN_CHUNKS, SCAN_COLS, 1, SSM_GROUP)
        rep = jnp.broadcast_to(blk, (DEPTH, SCAN_CHUNKS, SCAN_COLS, gpc, SSM_GROUP))
        return jnp.where(mask_c, rep.reshape(DEPTH, SCAN_CHUNKS, SCAN_COLS, MXU_DIM), 0.0)

    wc = jnp.concatenate([c_rows(c_re), c_rows(-c_im)], axis=2).astype(BF16)
    a_re = ab_r.reshape(DEPTH, 1, STATE_COLS)
    a_im = ab_i.reshape(DEPTH, 1, STATE_COLS)
    return wb, a_re, a_im, wc


def kernel(x, c, w_ada, b_ada, pre_mix_g, w_in, attn_sinks, lam_re, lam_im, log_dt, b_re, b_im,
           c_re, c_im, d_skip, w_glu, b_glu, attn_out_g, ssm_out_g, w_out, post_mix_g, pre_mlp_g,
           w_mlp_in, w_mlp_out, post_mlp_g):
    row = lambda a: a.reshape(DEPTH, 1, a.shape[-1])
    mod = _modulation(c, w_ada, b_ada).reshape(DEPTH, BATCH, N_MOD, D_MODEL)
    wb, a_re, a_im, wc = _ssm_params(lam_re, lam_im, log_dt, b_re, b_im, c_re, c_im)
    w1_b = w_mlp_in.astype(BF16)
    w2_b = w_mlp_out.astype(BF16)
    sinks = row(attn_sinks)
    for layer in range(DEPTH):
        q, kv, ssm_tm = _mix(layer, x, mod, row(pre_mix_g), w_in, wb, a_re, a_im, wc,
                             row(d_skip), w_glu, row(b_glu), row(ssm_out_g))
        x = _attn(layer, q, kv, ssm_tm, x, mod, sinks,
                  row(attn_out_g), w_out, row(post_mix_g))
        x = _mlp(layer, x, mod, row(pre_mlp_g), w1_b, w2_b, row(post_mlp_g))
    return x
```

```python
import functools

import jax
import jax.numpy as jnp
from jax import lax
from jax.experimental import pallas as pl
from jax.experimental.pallas import tpu as pltpu

D_MODEL = 1024
BATCH = 8
SEQ = 4096
DEPTH = 4
ATTN_WIDTH = 512
SSM_WIDTH = 512
HEAD_DIM = 64
N_Q_HEADS = 8
N_KV_HEADS = 2
Q_PER_KV = 4
KV_WIDTH = 128
WINDOW = 128
SSM_GROUP = 16
N_SSM_GROUPS = 32
STATE = 64
D_FF = 4096
IN_WIDTH = 1280
N_MOD = 6
EPS = 1e-6
NEG_INF = -1e30

LANES = 128
SUBLANES = 8
MXU_DIM = 256

STATE_COLS = N_SSM_GROUPS * STATE
BF16 = jnp.bfloat16
F32 = jnp.float32

VMEM_LIMIT = 56 * 1024 * 1024

TT_SSM = 64
SSM_SUB = 2
TQ_ATT = 1024
ATT_SUB = 2
TM_MLP = 1024
FF_CHUNK = 1024
SCAN_COLS = LANES * STATE // SSM_GROUP
SCAN_CHUNKS = STATE_COLS // SCAN_COLS


def _rms(x, g):
    return x * lax.rsqrt(jnp.mean(x * x, axis=-1, keepdims=True) + EPS) * g


def _dot_f32_lhs(a, w):
    return lax.dot_general(a, w, (((1,), (0,)), ((), ())), preferred_element_type=F32)


def _mod_kernel(c_ref, w_ref, b_ref, o_ref):
    c = c_ref[...]
    ca = (c * jax.nn.sigmoid(c)).astype(BF16)
    o_ref[0] = jnp.dot(ca, w_ref[0].astype(BF16), preferred_element_type=F32) + b_ref[0]


def _modulation(c, w_ada, b_ada):
    nb = 1536
    return pl.pallas_call(
        _mod_kernel,
        grid=(DEPTH, N_MOD * D_MODEL // nb),
        in_specs=[
            pl.BlockSpec((BATCH, D_MODEL), lambda l, j: (0, 0)),
            pl.BlockSpec((1, D_MODEL, nb), lambda l, j: (l, 0, j)),
            pl.BlockSpec((1, 1, nb), lambda l, j: (l, 0, j)),
        ],
        out_specs=pl.BlockSpec((1, BATCH, nb), lambda l, j: (l, 0, j)),
        out_shape=jax.ShapeDtypeStruct((DEPTH, BATCH, N_MOD * D_MODEL), F32),
        compiler_params=pltpu.CompilerParams(
            dimension_semantics=("arbitrary", "arbitrary"), vmem_limit_bytes=VMEM_LIMIT),
        name="adaln_mod",
    )(c, w_ada, b_ada.reshape(DEPTH, 1, N_MOD * D_MODEL))


TT_MIX = SSM_SUB * TT_SSM


def _mix_kernel(x_ref, mod_ref, pg_ref, win_ref, wb_ref, are_ref, aim_ref, wc_ref, d_ref, wglu_ref,
                bglu_ref, g_ref, q_ref, kv_ref, o_ref,
                hn_ref, winb_ref, uall_ref, utb_ref, otb_ref, state_ref, wglub_ref, *hs_refs):
    @pl.when(pl.program_id(0) == 0)
    def _():
        state_ref[...] = jnp.zeros_like(state_ref)
        wglub_ref[...] = wglu_ref[0].astype(BF16)
        winb_ref[...] = win_ref[0].astype(BF16)

    n_slab = SSM_WIDTH // LANES
    tiles_per_half = SCAN_COLS // MXU_DIM
    per = SCAN_CHUNKS // (SSM_WIDTH // MXU_DIM)
    u_col = ATTN_WIDTH + 2 * KV_WIDTH

    half = BATCH // 2
    for part in range(2):
        for b in range(part * half, (part + 1) * half):
            mod = mod_ref[0, b]
            hn_ref[b * TT_MIX:(b + 1) * TT_MIX, :] = (
                _rms(x_ref[b], pg_ref[0] * (1.0 + mod[1:2])) + mod[0:1]).astype(BF16)
        rows = slice(part * half * TT_MIX, (part + 1) * half * TT_MIX)
        uall_ref[rows, :] = jnp.dot(hn_ref[rows, :], winb_ref[:, u_col:], preferred_element_type=F32)

    for sub in range(SSM_SUB):
        for b in range(BATCH):
            r0 = b * TT_MIX + sub * TT_SSM
            for s in range(n_slab):
                utb_ref[sub * n_slab + s, pl.ds(b, TT_SSM, stride=BATCH), :] = (
                    uall_ref[r0:r0 + TT_SSM, s * LANES:(s + 1) * LANES])

    def qkv_piece(c):
        if c == 0:
            q = jnp.dot(hn_ref[...], winb_ref[:, :ATTN_WIDTH], preferred_element_type=F32)
            q_ref[...] = (q * (HEAD_DIM ** -0.5)).astype(BF16).reshape(BATCH, TT_MIX, ATTN_WIDTH)
        else:
            kv = jnp.dot(hn_ref[...], winb_ref[:, ATTN_WIDTH:u_col], preferred_element_type=F32)
            k = kv[:, :KV_WIDTH]
            v = kv[:, KV_WIDTH:]
            shape = (BATCH, TT_MIX, KV_WIDTH)
            kv_ref[:, :, 0 * KV_WIDTH:1 * KV_WIDTH] = k.astype(BF16).reshape(shape)
            kv_ref[:, :, 1 * KV_WIDTH:2 * KV_WIDTH] = v.astype(BF16).reshape(shape)
            kv_ref[:, :, 2 * KV_WIDTH:3 * KV_WIDTH] = (
                pltpu.roll(k, HEAD_DIM, 1).astype(BF16).reshape(shape))
            kv_ref[:, :, 3 * KV_WIDTH:4 * KV_WIDTH] = (
                pltpu.roll(v, HEAD_DIM, 1).astype(BF16).reshape(shape))


    def project_in(sub, k):
        ub = utb_ref[sub * n_slab + k]
        hs = hs_refs[sub * SCAN_CHUNKS + k]
        for part in range(2):
            for j in range(tiles_per_half):
                n = part * (STATE_COLS // MXU_DIM) + k * tiles_per_half + j
                c0 = part * SCAN_COLS + j * MXU_DIM
                hs[:, c0:c0 + MXU_DIM] = _dot_f32_lhs(ub, wb_ref[0, n])

    def scan(sub, k):
        hs = hs_refs[sub * SCAN_CHUNKS + k]
        cols = slice(k * SCAN_COLS, (k + 1) * SCAN_COLS)
        ar = jnp.broadcast_to(are_ref[0, :, cols], (BATCH, SCAN_COLS))
        ai = jnp.broadcast_to(aim_ref[0, :, cols], (BATCH, SCAN_COLS))
        hr = state_ref[k, :, :SCAN_COLS]
        hi = state_ref[k, :, SCAN_COLS:]
        for t in range(TT_SSM):
            rows = slice(t * BATCH, (t + 1) * BATCH)
            nr = ar * hr - ai * hi + hs[rows, :SCAN_COLS]
            ni = ar * hi + ai * hr + hs[rows, SCAN_COLS:]
            hs[rows, :SCAN_COLS] = nr
            hs[rows, SCAN_COLS:] = ni
            hr, hi = nr, ni
        state_ref[k, :, :SCAN_COLS] = hr
        state_ref[k, :, SCAN_COLS:] = hi

    def project_out(sub, n):
        chunks = range(n * per, (n + 1) * per)
        h = jnp.concatenate([hs_refs[sub * SCAN_CHUNKS + k][...] for k in chunks], axis=1)
        w = wc_ref[0, n * per:(n + 1) * per].reshape(per * 2 * SCAN_COLS, MXU_DIM)
        return _dot_f32_lhs(h, w)

    def finish(sub, ys):
        u = jnp.concatenate([utb_ref[sub * n_slab + s] for s in range(n_slab)], axis=1)
        y = jnp.concatenate(ys, axis=1) + d_ref[0] * u
        z = jax.nn.gelu(y)
        gate = jax.nn.sigmoid(
            jnp.dot(z.astype(BF16), wglub_ref[...], preferred_element_type=F32) + bglu_ref[0])
        res = _rms(z * gate, g_ref[0])
        for s in range(n_slab):
            otb_ref[sub * n_slab + s] = res[:, s * LANES:(s + 1) * LANES]
        t0 = sub * TT_SSM
        for b in range(BATCH):
            for s in range(n_slab):
                c0 = b * SSM_WIDTH + s * LANES
                o_ref[t0:t0 + TT_SSM, c0:c0 + LANES] = (
                    otb_ref[sub * n_slab + s, pl.ds(b, TT_SSM, stride=BATCH), :].astype(BF16))

    pending = None
    qkv = [0, 1]
    for sub in range(SSM_SUB):
        project_in(sub, 0)
        for k in range(SCAN_CHUNKS):
            if k + 1 < SCAN_CHUNKS:
                project_in(sub, k + 1)
            if pending is not None and k < len(pending[1]):
                pending[2].append(project_out(pending[0], pending[1][k]))
            elif qkv and k % 2 == 1:
                qkv_piece(qkv.pop(0))
            scan(sub, k)
        if pending is not None:
            finish(pending[0], pending[2])
        pending = (sub, list(range(SSM_WIDTH // MXU_DIM)), [])
    finish(pending[0], [project_out(pending[0], n) for n in pending[1]])
    for c in qkv:
        qkv_piece(c)


def _mix(layer, x, mod, pre_g, w_in, wb, a_re, a_im, wc, d_skip, w_glu, b_glu, ssm_g):
    rows = TT_SSM * BATCH
    vec = lambda n: pl.BlockSpec((1, 1, n), lambda i: (layer, 0, 0))
    return pl.pallas_call(
        _mix_kernel,
        grid=(SEQ // TT_MIX,),
        in_specs=[
            pl.BlockSpec((BATCH, TT_MIX, D_MODEL), lambda i: (0, i, 0)),
            pl.BlockSpec((1, BATCH, N_MOD, D_MODEL), lambda i: (layer, 0, 0, 0)),
            vec(D_MODEL),
            pl.BlockSpec((1, D_MODEL, IN_WIDTH), lambda i: (layer, 0, 0)),
            pl.BlockSpec((1,) + wb.shape[1:], lambda i: (layer, 0, 0, 0)),
            vec(STATE_COLS), vec(STATE_COLS),
            pl.BlockSpec((1,) + wc.shape[1:], lambda i: (layer, 0, 0, 0)),
            vec(SSM_WIDTH),
            pl.BlockSpec((1, SSM_WIDTH, SSM_WIDTH), lambda i: (layer, 0, 0)),
            vec(SSM_WIDTH), vec(SSM_WIDTH),
        ],
        out_specs=[
            pl.BlockSpec((BATCH, TT_MIX, ATTN_WIDTH), lambda i: (0, i, 0)),
            pl.BlockSpec((BATCH, TT_MIX, 4 * KV_WIDTH), lambda i: (0, i, 0)),
            pl.BlockSpec((TT_MIX, BATCH * SSM_WIDTH), lambda i: (i, 0)),
        ],
        out_shape=[
            jax.ShapeDtypeStruct((BATCH, SEQ, ATTN_WIDTH), BF16),
            jax.ShapeDtypeStruct((BATCH, SEQ, 4 * KV_WIDTH), BF16),
            jax.ShapeDtypeStruct((SEQ, BATCH * SSM_WIDTH), BF16),
        ],
        scratch_shapes=[
            pltpu.VMEM((BATCH * TT_MIX, D_MODEL), BF16),
            pltpu.VMEM((D_MODEL, IN_WIDTH), BF16),
            pltpu.VMEM((BATCH * TT_MIX, SSM_WIDTH), F32),
            pltpu.VMEM((SSM_SUB * SSM_WIDTH // LANES, rows, LANES), F32),
            pltpu.VMEM((SSM_SUB * SSM_WIDTH // LANES, rows, LANES), F32),
            pltpu.VMEM((SCAN_CHUNKS, BATCH, 2 * SCAN_COLS), F32),
            pltpu.VMEM((SSM_WIDTH, SSM_WIDTH), BF16),
        ] + [pltpu.VMEM((rows, 2 * SCAN_COLS), F32) for _ in range(SSM_SUB * SCAN_CHUNKS)],
        compiler_params=pltpu.CompilerParams(
            dimension_semantics=("arbitrary",), vmem_limit_bytes=VMEM_LIMIT),
        name="mix_in_s5",
    )(x, mod, pre_g, w_in, wb, a_re, a_im, wc, d_skip, w_glu, b_glu, ssm_g)


def _attn_kernel(q_ref, kvc_ref, kvp_ref, ssm_ref, x_ref, mod_ref, sink_ref, ag_ref, wout_ref,
                 pg_ref, w1_ref, w2_ref, o_ref, w1b_ref, w2b_ref, kpad_ref, vpad_ref, bias_ref,
                 heads_ref, st_ref, p_ref, attn_ref, mixed_ref, woutb_ref):
    i = pl.program_id(1)
    nq = TQ_ATT // WINDOW

    w1b_ref[0] = w1_ref[0].astype(BF16)
    w2b_ref[0] = w2_ref[0].astype(BF16)

    @pl.when((pl.program_id(0) == 0) & (i == 0))
    def _():
        j = lax.broadcasted_iota(jnp.int32, (2 * WINDOW, WINDOW), 0)
        r = lax.broadcasted_iota(jnp.int32, (2 * WINDOW, WINDOW), 1)
        diff = WINDOW + r - j
        valid = (diff >= 0) & (diff < WINDOW)
        for h in range(N_Q_HEADS):
            slope = 2.0 ** (-8.0 * (h + 1) / N_Q_HEADS)
            bias = -slope * diff.astype(F32)
            bias_ref[h] = jnp.where(valid, bias, NEG_INF)
            bias_ref[N_Q_HEADS + h] = jnp.where(valid & (j >= WINDOW), bias, NEG_INF)
        woutb_ref[...] = wout_ref[0].astype(BF16)

    lane = lax.broadcasted_iota(jnp.int32, (1, LANES), 1)
    for src, rows in ((kvp_ref, slice(0, WINDOW)), (kvc_ref, slice(WINDOW, WINDOW + TQ_ATT))):
        for hk in range(N_KV_HEADS):
            for par in range(2):
                keep = (lane >= par * HEAD_DIM) & (lane < (par + 1) * HEAD_DIM)
                off = 0 if par == hk else 2 * KV_WIDTH
                kpad_ref[hk * 2 + par, rows, :] = jnp.where(
                    keep, src[0, :, off:off + KV_WIDTH], jnp.zeros((), BF16))
                vpad_ref[hk * 2 + par, rows, :] = jnp.where(
                    keep, src[0, :, off + KV_WIDTH:off + 2 * KV_WIDTH], jnp.zeros((), BF16))

    nt = (((1,), (1,)), ((), ()))
    tn = (((0,), (0,)), ((), ()))

    def scores(jb):
        q0 = jb * WINDOW
        for hk in range(N_KV_HEADS):
            qq = jnp.concatenate(
                [q_ref[0, q0:q0 + WINDOW, (2 * hk + l2) * LANES:(2 * hk + l2 + 1) * LANES]
                 for l2 in range(2)], axis=0)
            for par in range(2):
                kp = kpad_ref[hk * 2 + par, q0:q0 + 2 * WINDOW, :]
                st_ref[(jb % 2) * 2 * N_KV_HEADS + hk * 2 + par] = lax.dot_general(
                    kp, qq, nt, preferred_element_type=F32)

    def softmax(jb):
        table = jnp.where(i == 0, N_Q_HEADS, 0) if jb == 0 else 0
        for hk in range(N_KV_HEADS):
            for par in range(2):
                for l2 in range(2):
                    h = Q_PER_KV * hk + 2 * l2 + par
                    s = (st_ref[(jb % 2) * 2 * N_KV_HEADS + hk * 2 + par, :,
                                l2 * WINDOW:(l2 + 1) * WINDOW] + bias_ref[table + h])
                    sink = sink_ref[0, 0, h]
                    m = jnp.maximum(jnp.max(s, axis=0, keepdims=True), sink)
                    e = jnp.exp(s - m)
                    denom = jnp.sum(e, axis=0, keepdims=True) + jnp.exp(sink - m)
                    p_ref[(jb % 2) * N_Q_HEADS + h] = (e * (1.0 / denom)).astype(BF16)

    def values(jb):
        q0 = jb * WINDOW
        for lt in range(N_Q_HEADS // 2):
            hk = 2 * lt // Q_PER_KV
            acc = None
            for par in range(2):
                vp = vpad_ref[hk * 2 + par, q0:q0 + 2 * WINDOW, :]
                o = lax.dot_general(p_ref[(jb % 2) * N_Q_HEADS + 2 * lt + par], vp, tn,
                                    preferred_element_type=F32)
                acc = o if acc is None else acc + o
            heads_ref[q0:q0 + WINDOW, lt * LANES:(lt + 1) * LANES] = acc

    mod = mod_ref[0, 0]
    gate = mod[2:3] * pg_ref[0]

    def out_proj_pieces(part):
        rows = slice(part * (TQ_ATT // ATT_SUB), (part + 1) * (TQ_ATT // ATT_SUB))
        n_col = D_MODEL // MXU_DIM

        def piece(c):
            if c == 0:
                attn_ref[rows, :] = _rms(heads_ref[rows, :], ag_ref[0]).astype(BF16)
            cols = slice(c * MXU_DIM, (c + 1) * MXU_DIM)
            heads = jnp.concatenate([attn_ref[rows, :], ssm_ref[rows, :]], axis=1)
            mixed_ref[rows, cols] = jnp.dot(heads, woutb_ref[:, cols], preferred_element_type=F32)
            if c == n_col - 1:
                o_ref[0, rows, :] = x_ref[0, rows, :] + _rms(mixed_ref[rows, :], gate)

        return [functools.partial(piece, c) for c in range(n_col)]

    per_part = nq // ATT_SUB
    queue = []
    scores(0)
    for jb in range(nq):
        if jb + 1 < nq:
            scores(jb + 1)
        softmax(jb)
        values(jb)
        for _ in range(-(-len(queue) // (per_part - jb % per_part))):
            queue.pop(0)()
        if (jb + 1) % per_part == 0:
            queue.extend(out_proj_pieces(jb // per_part))
    for piece in queue:
        piece()


def _attn(layer, q, kv, ssm_tm, x, mod, sinks, attn_g, w_out, post_g, w1, w2):
    nt = SEQ // TQ_ATT
    per_tile = TQ_ATT // WINDOW
    steps = BATCH * nt
    vec = lambda n: pl.BlockSpec((1, 1, n), lambda b, i: (layer, 0, 0))
    slab = lambda rows, cols: pl.BlockSpec(
        (1, rows // steps, cols), lambda b, i: (layer, b * nt + i, 0))
    slab_out = lambda rows, cols: pl.BlockSpec(
        (1, rows // steps, cols), lambda b, i: (0, b * nt + i, 0))
    return pl.pallas_call(
        _attn_kernel,
        grid=(BATCH, nt),
        in_specs=[
            pl.BlockSpec((1, TQ_ATT, ATTN_WIDTH), lambda b, i: (b, i, 0)),
            pl.BlockSpec((1, TQ_ATT, 4 * KV_WIDTH), lambda b, i: (b, i, 0)),
            pl.BlockSpec((1, WINDOW, 4 * KV_WIDTH),
                         lambda b, i: (b, jnp.maximum(i * per_tile - 1, 0), 0)),
            pl.BlockSpec((TQ_ATT, SSM_WIDTH), lambda b, i: (i, b)),
            pl.BlockSpec((1, TQ_ATT, D_MODEL), lambda b, i: (b, i, 0)),
            pl.BlockSpec((1, 1, N_MOD, D_MODEL), lambda b, i: (layer, b, 0, 0)),
            pl.BlockSpec((1, 1, N_Q_HEADS), lambda b, i: (layer, 0, 0), memory_space=pltpu.SMEM),
            vec(ATTN_WIDTH),
            pl.BlockSpec((1, D_MODEL, D_MODEL), lambda b, i: (layer, 0, 0)),
            vec(D_MODEL),
            slab(D_MODEL, D_FF),
            slab(D_FF, D_MODEL),
        ],
        out_specs=[
            pl.BlockSpec((1, TQ_ATT, D_MODEL), lambda b, i: (b, i, 0)),
            slab_out(D_MODEL, D_FF),
            slab_out(D_FF, D_MODEL),
        ],
        out_shape=[
            jax.ShapeDtypeStruct((BATCH, SEQ, D_MODEL), F32),
            jax.ShapeDtypeStruct((1, D_MODEL, D_FF), BF16),
            jax.ShapeDtypeStruct((1, D_FF, D_MODEL), BF16),
        ],
        scratch_shapes=[
            pltpu.VMEM((2 * N_KV_HEADS, WINDOW + TQ_ATT, LANES), BF16),
            pltpu.VMEM((2 * N_KV_HEADS, WINDOW + TQ_ATT, LANES), BF16),
            pltpu.VMEM((2 * N_Q_HEADS, 2 * WINDOW, WINDOW), F32),
            pltpu.VMEM((TQ_ATT, ATTN_WIDTH), F32),
            pltpu.VMEM((2 * 2 * N_KV_HEADS, 2 * WINDOW, 2 * WINDOW), F32),
            pltpu.VMEM((2 * N_Q_HEADS, 2 * WINDOW, WINDOW), BF16),
            pltpu.VMEM((TQ_ATT, ATTN_WIDTH), BF16),
            pltpu.VMEM((TQ_ATT, D_MODEL), F32),
            pltpu.VMEM((D_MODEL, D_MODEL), BF16),
        ],
        compiler_params=pltpu.CompilerParams(
            dimension_semantics=("arbitrary", "arbitrary"), vmem_limit_bytes=VMEM_LIMIT),
        name="attn_outproj",
    )(q, kv, kv, ssm_tm, x, mod, sinks, attn_g, w_out, post_g, w1, w2)


def _mlp_kernel(x_ref, mod_ref, g_ref, w1_ref, w2_ref, pg_ref, o_ref, acc_ref):
    x = x_ref[0]
    mod = mod_ref[0, 0]
    h = (_rms(x, g_ref[0] * (1.0 + mod[4:5])) + mod[3:4]).astype(BF16)
    for c in range(D_FF // FF_CHUNK):
        cols = slice(c * FF_CHUNK, (c + 1) * FF_CHUNK)
        f = jnp.dot(h, w1_ref[0, :, cols], preferred_element_type=F32)
        f = jnp.square(jnp.maximum(f, 0.0)).astype(BF16)
        part = jnp.dot(f, w2_ref[0, cols, :], preferred_element_type=F32)
        if c == 0:
            acc_ref[...] = part
        else:
            acc_ref[...] += part
    o_ref[0] = x + _rms(acc_ref[...], mod[5:6] * pg_ref[0])


def _mlp(layer, x, mod, pre_g, w1, w2, post_g):
    nt = SEQ // TM_MLP
    vec = lambda n: pl.BlockSpec((1, 1, n), lambda b, i: (layer, 0, 0))
    return pl.pallas_call(
        _mlp_kernel,
        grid=(BATCH, nt),
        in_specs=[
            pl.BlockSpec((1, TM_MLP, D_MODEL), lambda b, i: (b, i, 0)),
            pl.BlockSpec((1, 1, N_MOD, D_MODEL), lambda b, i: (layer, b, 0, 0)),
            vec(D_MODEL),
            pl.BlockSpec((1, D_MODEL, D_FF), lambda b, i: (0, 0, 0), pipeline_mode=pl.Buffered(1)),
            pl.BlockSpec((1, D_FF, D_MODEL), lambda b, i: (0, 0, 0), pipeline_mode=pl.Buffered(1)),
            vec(D_MODEL),
        ],
        out_specs=pl.BlockSpec((1, TM_MLP, D_MODEL), lambda b, i: (b, i, 0)),
        out_shape=jax.ShapeDtypeStruct((BATCH, SEQ, D_MODEL), F32),
        scratch_shapes=[pltpu.VMEM((TM_MLP, D_MODEL), F32)],
        compiler_params=pltpu.CompilerParams(
            dimension_semantics=("arbitrary", "arbitrary"), vmem_limit_bytes=VMEM_LIMIT),
        name="mlp",
    )(x, mod, pre_g, w1, w2, post_g)


def _ssm_params(lam_re, lam_im, log_dt, b_re, b_im, c_re, c_im):
    dt = jnp.exp(log_dt)[..., None]
    mag = jnp.exp(lam_re * dt)
    ang = lam_im * dt
    ab_r = mag * jnp.cos(ang)
    ab_i = mag * jnp.sin(ang)
    nr = ab_r - 1.0
    ni = ab_i
    den = lam_re * lam_re + lam_im * lam_im
    f_r = (nr * lam_re + ni * lam_im) / den
    f_i = (ni * lam_re - nr * lam_im) / den
    bb_r = f_r[..., None] * b_re - f_i[..., None] * b_im
    bb_i = f_r[..., None] * b_im + f_i[..., None] * b_re
    gpt = MXU_DIM // STATE
    gps = LANES // SSM_GROUP
    n_t = STATE_COLS // MXU_DIM
    shape_b = (n_t, LANES, MXU_DIM)
    n_i = lax.broadcasted_iota(jnp.int32, shape_b, 0)
    g_i = lax.broadcasted_iota(jnp.int32, shape_b, 1) // SSM_GROUP
    h_i = lax.broadcasted_iota(jnp.int32, shape_b, 2) // STATE
    mask_b = g_i == gpt * (n_i % (gps // gpt)) + h_i

    def b_tiles(bb):
        blk = bb.reshape(DEPTH, n_t, gpt, STATE, SSM_GROUP).transpose(0, 1, 4, 2, 3)
        blk = blk.reshape(DEPTH, n_t, 1, SSM_GROUP, MXU_DIM)
        rep = jnp.broadcast_to(blk, (DEPTH, n_t, gps, SSM_GROUP, MXU_DIM))
        return jnp.where(mask_b, rep.reshape(DEPTH, n_t, LANES, MXU_DIM), 0.0)

    wb = jnp.concatenate([b_tiles(bb_r), b_tiles(bb_i)], axis=1).astype(BF16)

    gpc = MXU_DIM // SSM_GROUP
    shape_c = (SCAN_CHUNKS, SCAN_COLS, MXU_DIM)
    k_i = lax.broadcasted_iota(jnp.int32, shape_c, 0)
    r_i = lax.broadcasted_iota(jnp.int32, shape_c, 1) // STATE
    c_i = lax.broadcasted_iota(jnp.int32, shape_c, 2) // SSM_GROUP
    mask_c = c_i == (gps * k_i + r_i) % gpc

    def c_rows(cc):
        blk = cc.transpose(0, 1, 3, 2).reshape(DEPTH, SCAN_CHUNKS, SCAN_COLS, 1, SSM_GROUP)
        rep = jnp.broadcast_to(blk, (DEPTH, SCAN_CHUNKS, SCAN_COLS, gpc, SSM_GROUP))
        return jnp.where(mask_c, rep.reshape(DEPTH, SCAN_CHUNKS, SCAN_COLS, MXU_DIM), 0.0)

    wc = jnp.concatenate([c_rows(c_re), c_rows(-c_im)], axis=2).astype(BF16)
    a_re = ab_r.reshape(DEPTH, 1, STATE_COLS)
    a_im = ab_i.reshape(DEPTH, 1, STATE_COLS)
    return wb, a_re, a_im, wc


def kernel(x, c, w_ada, b_ada, pre_mix_g, w_in, attn_sinks, lam_re, lam_im, log_dt, b_re, b_im,
           c_re, c_im, d_skip, w_glu, b_glu, attn_out_g, ssm_out_g, w_out, post_mix_g, pre_mlp_g,
           w_mlp_in, w_mlp_out, post_mlp_g):
    row = lambda a: a.reshape(DEPTH, 1, a.shape[-1])
    mod = _modulation(c, w_ada, b_ada).reshape(DEPTH, BATCH, N_MOD, D_MODEL)
    wb, a_re, a_im, wc = _ssm_params(lam_re, lam_im, log_dt, b_re, b_im, c_re, c_im)
    sinks = row(attn_sinks)
    for layer in range(DEPTH):
        q, kv, ssm_tm = _mix(layer, x, mod, row(pre_mix_g), w_in, wb, a_re, a_im, wc,
                             row(d_skip), w_glu, row(b_glu), row(ssm_out_g))
        x, w1_b, w2_b = _attn(layer, q, kv, ssm_tm, x, mod, sinks, row(attn_out_g), w_out,
                              row(post_mix_g), w_mlp_in, w_mlp_out)
        x = _mlp(layer, x, mod, row(pre_mlp_g), w1_b, w2_b, row(post_mlp_g))
    return x
```

```python
import functools

import jax
import jax.numpy as jnp
from jax import lax
from jax.experimental import pallas as pl
from jax.experimental.pallas import tpu as pltpu

D_MODEL = 1024
BATCH = 8
SEQ = 4096
DEPTH = 4
ATTN_WIDTH = 512
SSM_WIDTH = 512
HEAD_DIM = 64
N_Q_HEADS = 8
N_KV_HEADS = 2
Q_PER_KV = 4
KV_WIDTH = 128
WINDOW = 128
SSM_GROUP = 16
N_SSM_GROUPS = 32
STATE = 64
D_FF = 4096
IN_WIDTH = 1280
N_MOD = 6
EPS = 1e-6
NEG_INF = -1e30
LOG2E = 1.4426950408889634

LANES = 128
SUBLANES = 8
MXU_DIM = 256

STATE_COLS = N_SSM_GROUPS * STATE
BF16 = jnp.bfloat16
F32 = jnp.float32

VMEM_LIMIT = 56 * 1024 * 1024

TT_SSM = 64
SSM_SUB = 2
TQ_ATT = 1024
ATT_SUB = 2
TM_MLP = 1024
FF_CHUNK = 1024
SCAN_COLS = LANES * STATE // SSM_GROUP
SCAN_CHUNKS = STATE_COLS // SCAN_COLS


def _rms(x, g):
    return x * lax.rsqrt(jnp.mean(x * x, axis=-1, keepdims=True) + EPS) * g


def _dot_f32_lhs(a, w):
    return lax.dot_general(a, w, (((1,), (0,)), ((), ())), preferred_element_type=F32)


def _mod_kernel(c_ref, w_ref, b_ref, o_ref):
    c = c_ref[...]
    ca = (c * jax.nn.sigmoid(c)).astype(BF16)
    o_ref[0] = jnp.dot(ca, w_ref[0].astype(BF16), preferred_element_type=F32) + b_ref[0]


def _modulation(c, w_ada, b_ada):
    nb = 1536
    return pl.pallas_call(
        _mod_kernel,
        grid=(DEPTH, N_MOD * D_MODEL // nb),
        in_specs=[
            pl.BlockSpec((BATCH, D_MODEL), lambda l, j: (0, 0)),
            pl.BlockSpec((1, D_MODEL, nb), lambda l, j: (l, 0, j)),
            pl.BlockSpec((1, 1, nb), lambda l, j: (l, 0, j)),
        ],
        out_specs=pl.BlockSpec((1, BATCH, nb), lambda l, j: (l, 0, j)),
        out_shape=jax.ShapeDtypeStruct((DEPTH, BATCH, N_MOD * D_MODEL), F32),
        compiler_params=pltpu.CompilerParams(
            dimension_semantics=("arbitrary", "arbitrary"), vmem_limit_bytes=VMEM_LIMIT),
        name="adaln_mod",
    )(c, w_ada, b_ada.reshape(DEPTH, 1, N_MOD * D_MODEL))


TT_MIX = SSM_SUB * TT_SSM


def _mix_kernel(x_ref, mod_ref, pg_ref, win_ref, wb_ref, are_ref, aim_ref, wc_ref, d_ref, wglu_ref,
                bglu_ref, g_ref, q_ref, kv_ref, o_ref,
                hn_ref, winb_ref, uall_ref, utb_ref, otb_ref, state_ref, wglub_ref, *hs_refs):
    @pl.when(pl.program_id(0) == 0)
    def _():
        state_ref[...] = jnp.zeros_like(state_ref)
        wglub_ref[...] = wglu_ref[0].astype(BF16)
        winb_ref[...] = win_ref[0].astype(BF16)

    n_slab = SSM_WIDTH // LANES
    tiles_per_half = SCAN_COLS // MXU_DIM
    per = SCAN_CHUNKS // (SSM_WIDTH // MXU_DIM)
    u_col = ATTN_WIDTH + 2 * KV_WIDTH

    half = BATCH // 2
    for part in range(2):
        for b in range(part * half, (part + 1) * half):
            mod = mod_ref[0, b]
            hn_ref[b * TT_MIX:(b + 1) * TT_MIX, :] = (
                _rms(x_ref[b], pg_ref[0] * (1.0 + mod[1:2])) + mod[0:1]).astype(BF16)
        rows = slice(part * half * TT_MIX, (part + 1) * half * TT_MIX)
        uall_ref[rows, :] = jnp.dot(hn_ref[rows, :], winb_ref[:, u_col:], preferred_element_type=F32)

    for sub in range(SSM_SUB):
        for b in range(BATCH):
            r0 = b * TT_MIX + sub * TT_SSM
            for s in range(n_slab):
                utb_ref[sub * n_slab + s, pl.ds(b, TT_SSM, stride=BATCH), :] = (
                    uall_ref[r0:r0 + TT_SSM, s * LANES:(s + 1) * LANES])

    def qkv_piece(c):
        if c == 0:
            q = jnp.dot(hn_ref[...], winb_ref[:, :ATTN_WIDTH], preferred_element_type=F32)
            q_ref[...] = (q * (HEAD_DIM ** -0.5 * LOG2E)).astype(BF16).reshape(
                BATCH, TT_MIX, ATTN_WIDTH)
        else:
            kv = jnp.dot(hn_ref[...], winb_ref[:, ATTN_WIDTH:u_col], preferred_element_type=F32)
            k = kv[:, :KV_WIDTH]
            v = kv[:, KV_WIDTH:]
            shape = (BATCH, TT_MIX, KV_WIDTH)
            kv_ref[:, :, 0 * KV_WIDTH:1 * KV_WIDTH] = k.astype(BF16).reshape(shape)
            kv_ref[:, :, 1 * KV_WIDTH:2 * KV_WIDTH] = v.astype(BF16).reshape(shape)
            kv_ref[:, :, 2 * KV_WIDTH:3 * KV_WIDTH] = (
                pltpu.roll(k, HEAD_DIM, 1).astype(BF16).reshape(shape))
            kv_ref[:, :, 3 * KV_WIDTH:4 * KV_WIDTH] = (
                pltpu.roll(v, HEAD_DIM, 1).astype(BF16).reshape(shape))


    def project_in(sub, k):
        ub = utb_ref[sub * n_slab + k]
        hs = hs_refs[sub * SCAN_CHUNKS + k]
        for part in range(2):
            for j in range(tiles_per_half):
                n = part * (STATE_COLS // MXU_DIM) + k * tiles_per_half + j
                c0 = part * SCAN_COLS + j * MXU_DIM
                hs[:, c0:c0 + MXU_DIM] = _dot_f32_lhs(ub, wb_ref[0, n])

    def scan(sub, k):
        hs = hs_refs[sub * SCAN_CHUNKS + k]
        cols = slice(k * SCAN_COLS, (k + 1) * SCAN_COLS)
        ar = jnp.broadcast_to(are_ref[0, :, cols], (BATCH, SCAN_COLS))
        ai = jnp.broadcast_to(aim_ref[0, :, cols], (BATCH, SCAN_COLS))
        hr = state_ref[k, :, :SCAN_COLS]
        hi = state_ref[k, :, SCAN_COLS:]
        for t in range(TT_SSM):
            rows = slice(t * BATCH, (t + 1) * BATCH)
            nr = ar * hr - ai * hi + hs[rows, :SCAN_COLS]
            ni = ar * hi + ai * hr + hs[rows, SCAN_COLS:]
            hs[rows, :SCAN_COLS] = nr
            hs[rows, SCAN_COLS:] = ni
            hr, hi = nr, ni
        state_ref[k, :, :SCAN_COLS] = hr
        state_ref[k, :, SCAN_COLS:] = hi

    def project_out(sub, n):
        chunks = range(n * per, (n + 1) * per)
        h = jnp.concatenate([hs_refs[sub * SCAN_CHUNKS + k][...] for k in chunks], axis=1)
        w = wc_ref[0, n * per:(n + 1) * per].reshape(per * 2 * SCAN_COLS, MXU_DIM)
        return _dot_f32_lhs(h, w)

    def finish(sub, ys):
        u = jnp.concatenate([utb_ref[sub * n_slab + s] for s in range(n_slab)], axis=1)
        y = jnp.concatenate(ys, axis=1) + d_ref[0] * u
        z = jax.nn.gelu(y)
        gate = jax.nn.sigmoid(
            jnp.dot(z.astype(BF16), wglub_ref[...], preferred_element_type=F32) + bglu_ref[0])
        res = _rms(z * gate, g_ref[0])
        for s in range(n_slab):
            otb_ref[sub * n_slab + s] = res[:, s * LANES:(s + 1) * LANES]
        t0 = sub * TT_SSM
        for b in range(BATCH):
            for s in range(n_slab):
                c0 = b * SSM_WIDTH + s * LANES
                o_ref[t0:t0 + TT_SSM, c0:c0 + LANES] = (
                    otb_ref[sub * n_slab + s, pl.ds(b, TT_SSM, stride=BATCH), :].astype(BF16))

    pending = None
    qkv = [0, 1]
    for sub in range(SSM_SUB):
        project_in(sub, 0)
        for k in range(SCAN_CHUNKS):
            if k + 1 < SCAN_CHUNKS:
                project_in(sub, k + 1)
            if pending is not None and k < len(pending[1]):
                pending[2].append(project_out(pending[0], pending[1][k]))
            elif qkv and k % 2 == 1:
                qkv_piece(qkv.pop(0))
            scan(sub, k)
        if pending is not None:
            finish(pending[0], pending[2])
        pending = (sub, list(range(SSM_WIDTH // MXU_DIM)), [])
    finish(pending[0], [project_out(pending[0], n) for n in pending[1]])
    for c in qkv:
        qkv_piece(c)


def _mix(layer, x, mod, pre_g, w_in, wb, a_re, a_im, wc, d_skip, w_glu, b_glu, ssm_g):
    rows = TT_SSM * BATCH
    vec = lambda n: pl.BlockSpec((1, 1, n), lambda i: (layer, 0, 0))
    return pl.pallas_call(
        _mix_kernel,
        grid=(SEQ // TT_MIX,),
        in_specs=[
            pl.BlockSpec((BATCH, TT_MIX, D_MODEL), lambda i: (0, i, 0)),
            pl.BlockSpec((1, BATCH, N_MOD, D_MODEL), lambda i: (layer, 0, 0, 0)),
            vec(D_MODEL),
            pl.BlockSpec((1, D_MODEL, IN_WIDTH), lambda i: (layer, 0, 0)),
            pl.BlockSpec((1,) + wb.shape[1:], lambda i: (layer, 0, 0, 0)),
            vec(STATE_COLS), vec(STATE_COLS),
            pl.BlockSpec((1,) + wc.shape[1:], lambda i: (layer, 0, 0, 0)),
            vec(SSM_WIDTH),
            pl.BlockSpec((1, SSM_WIDTH, SSM_WIDTH), lambda i: (layer, 0, 0)),
            vec(SSM_WIDTH), vec(SSM_WIDTH),
        ],
        out_specs=[
            pl.BlockSpec((BATCH, TT_MIX, ATTN_WIDTH), lambda i: (0, i, 0)),
            pl.BlockSpec((BATCH, TT_MIX, 4 * KV_WIDTH), lambda i: (0, i, 0)),
            pl.BlockSpec((TT_MIX, BATCH * SSM_WIDTH), lambda i: (i, 0)),
        ],
        out_shape=[
            jax.ShapeDtypeStruct((BATCH, SEQ, ATTN_WIDTH), BF16),
            jax.ShapeDtypeStruct((BATCH, SEQ, 4 * KV_WIDTH), BF16),
            jax.ShapeDtypeStruct((SEQ, BATCH * SSM_WIDTH), BF16),
        ],
        scratch_shapes=[
            pltpu.VMEM((BATCH * TT_MIX, D_MODEL), BF16),
            pltpu.VMEM((D_MODEL, IN_WIDTH), BF16),
            pltpu.VMEM((BATCH * TT_MIX, SSM_WIDTH), F32),
            pltpu.VMEM((SSM_SUB * SSM_WIDTH // LANES, rows, LANES), F32),
            pltpu.VMEM((SSM_SUB * SSM_WIDTH // LANES, rows, LANES), F32),
            pltpu.VMEM((SCAN_CHUNKS, BATCH, 2 * SCAN_COLS), F32),
            pltpu.VMEM((SSM_WIDTH, SSM_WIDTH), BF16),
        ] + [pltpu.VMEM((rows, 2 * SCAN_COLS), F32) for _ in range(SSM_SUB * SCAN_CHUNKS)],
        compiler_params=pltpu.CompilerParams(
            dimension_semantics=("arbitrary",), vmem_limit_bytes=VMEM_LIMIT),
        name="mix_in_s5",
    )(x, mod, pre_g, w_in, wb, a_re, a_im, wc, d_skip, w_glu, b_glu, ssm_g)


def _attn_kernel(q_ref, kvc_ref, kvp_ref, ssm_ref, x_ref, mod_ref, sink_ref, ag_ref, wout_ref,
                 pg_ref, w1_ref, w2_ref, o_ref, w1b_ref, w2b_ref, kpad_ref, vpad_ref, bias_ref,
                 heads_ref, st_ref, p_ref, attn_ref, mixed_ref, woutb_ref):
    i = pl.program_id(1)
    nq = TQ_ATT // WINDOW

    w1b_ref[0] = w1_ref[0].astype(BF16)
    w2b_ref[0] = w2_ref[0].astype(BF16)

    @pl.when((pl.program_id(0) == 0) & (i == 0))
    def _():
        j = lax.broadcasted_iota(jnp.int32, (2 * WINDOW, WINDOW), 0)
        r = lax.broadcasted_iota(jnp.int32, (2 * WINDOW, WINDOW), 1)
        diff = WINDOW + r - j
        valid = (diff >= 0) & (diff < WINDOW)
        for h in range(N_Q_HEADS):
            slope = 2.0 ** (-8.0 * (h + 1) / N_Q_HEADS)
            bias = -(slope * LOG2E) * diff.astype(F32)
            bias_ref[h] = jnp.where(valid, bias, NEG_INF)
            bias_ref[N_Q_HEADS + h] = jnp.where(valid & (j >= WINDOW), bias, NEG_INF)
        woutb_ref[...] = wout_ref[0].astype(BF16)

    lane = lax.broadcasted_iota(jnp.int32, (1, LANES), 1)
    for src, rows in ((kvp_ref, slice(0, WINDOW)), (kvc_ref, slice(WINDOW, WINDOW + TQ_ATT))):
        for hk in range(N_KV_HEADS):
            for par in range(2):
                keep = (lane >= par * HEAD_DIM) & (lane < (par + 1) * HEAD_DIM)
                off = 0 if par == hk else 2 * KV_WIDTH
                kpad_ref[hk * 2 + par, rows, :] = jnp.where(
                    keep, src[0, :, off:off + KV_WIDTH], jnp.zeros((), BF16))
                vpad_ref[hk * 2 + par, rows, :] = jnp.where(
                    keep, src[0, :, off + KV_WIDTH:off + 2 * KV_WIDTH], jnp.zeros((), BF16))

    nt = (((1,), (1,)), ((), ()))
    tn = (((0,), (0,)), ((), ()))

    def scores(jb):
        q0 = jb * WINDOW
        for hk in range(N_KV_HEADS):
            qq = jnp.concatenate(
                [q_ref[0, q0:q0 + WINDOW, (2 * hk + l2) * LANES:(2 * hk + l2 + 1) * LANES]
                 for l2 in range(2)], axis=0)
            for par in range(2):
                kp = kpad_ref[hk * 2 + par, q0:q0 + 2 * WINDOW, :]
                st_ref[(jb % 2) * 2 * N_KV_HEADS + hk * 2 + par] = lax.dot_general(
                    kp, qq, nt, preferred_element_type=F32)

    def softmax(jb):
        table = jnp.where(i == 0, N_Q_HEADS, 0) if jb == 0 else 0
        for hk in range(N_KV_HEADS):
            for par in range(2):
                for l2 in range(2):
                    h = Q_PER_KV * hk + 2 * l2 + par
                    s = (st_ref[(jb % 2) * 2 * N_KV_HEADS + hk * 2 + par, :,
                                l2 * WINDOW:(l2 + 1) * WINDOW] + bias_ref[table + h])
                    sink = sink_ref[0, 0, h] * LOG2E
                    m = jnp.maximum(jnp.max(s, axis=0, keepdims=True), sink)
                    e = jnp.exp2(s - m)
                    denom = jnp.sum(e, axis=0, keepdims=True) + jnp.exp2(sink - m)
                    p_ref[(jb % 2) * N_Q_HEADS + h] = (e * (1.0 / denom)).astype(BF16)

    def values(jb):
        q0 = jb * WINDOW
        for lt in range(N_Q_HEADS // 2):
            hk = 2 * lt // Q_PER_KV
            acc = None
            for par in range(2):
                vp = vpad_ref[hk * 2 + par, q0:q0 + 2 * WINDOW, :]
                o = lax.dot_general(p_ref[(jb % 2) * N_Q_HEADS + 2 * lt + par], vp, tn,
                                    preferred_element_type=F32)
                acc = o if acc is None else acc + o
            heads_ref[q0:q0 + WINDOW, lt * LANES:(lt + 1) * LANES] = acc

    mod = mod_ref[0, 0]
    gate = mod[2:3] * pg_ref[0]

    def out_proj_pieces(part):
        rows = slice(part * (TQ_ATT // ATT_SUB), (part + 1) * (TQ_ATT // ATT_SUB))
        n_col = D_MODEL // MXU_DIM

        def piece(c):
            if c == 0:
                attn_ref[rows, :] = _rms(heads_ref[rows, :], ag_ref[0]).astype(BF16)
            cols = slice(c * MXU_DIM, (c + 1) * MXU_DIM)
            heads = jnp.concatenate([attn_ref[rows, :], ssm_ref[rows, :]], axis=1)
            mixed_ref[rows, cols] = jnp.dot(heads, woutb_ref[:, cols], preferred_element_type=F32)
            if c == n_col - 1:
                o_ref[0, rows, :] = x_ref[0, rows, :] + _rms(mixed_ref[rows, :], gate)

        return [functools.partial(piece, c) for c in range(n_col)]

    per_part = nq // ATT_SUB
    queue = []
    scores(0)
    for jb in range(nq):
        if jb + 1 < nq:
            scores(jb + 1)
        softmax(jb)
        values(jb)
        for _ in range(-(-len(queue) // (per_part - jb % per_part))):
            queue.pop(0)()
        if (jb + 1) % per_part == 0:
            queue.extend(out_proj_pieces(jb // per_part))
    for piece in queue:
        piece()


def _attn(layer, q, kv, ssm_tm, x, mod, sinks, attn_g, w_out, post_g, w1, w2):
    nt = SEQ // TQ_ATT
    per_tile = TQ_ATT // WINDOW
    steps = BATCH * nt
    vec = lambda n: pl.BlockSpec((1, 1, n), lambda b, i: (layer, 0, 0))
    slab = lambda rows, cols: pl.BlockSpec(
        (1, rows // steps, cols), lambda b, i: (layer, b * nt + i, 0))
    slab_out = lambda rows, cols: pl.BlockSpec(
        (1, rows // steps, cols), lambda b, i: (0, b * nt + i, 0))
    return pl.pallas_call(
        _attn_kernel,
        grid=(BATCH, nt),
        in_specs=[
            pl.BlockSpec((1, TQ_ATT, ATTN_WIDTH), lambda b, i: (b, i, 0)),
            pl.BlockSpec((1, TQ_ATT, 4 * KV_WIDTH), lambda b, i: (b, i, 0)),
            pl.BlockSpec((1, WINDOW, 4 * KV_WIDTH),
                         lambda b, i: (b, jnp.maximum(i * per_tile - 1, 0), 0)),
            pl.BlockSpec((TQ_ATT, SSM_WIDTH), lambda b, i: (i, b)),
            pl.BlockSpec((1, TQ_ATT, D_MODEL), lambda b, i: (b, i, 0)),
            pl.BlockSpec((1, 1, N_MOD, D_MODEL), lambda b, i: (layer, b, 0, 0)),
            pl.BlockSpec((1, 1, N_Q_HEADS), lambda b, i: (layer, 0, 0), memory_space=pltpu.SMEM),
            vec(ATTN_WIDTH),
            pl.BlockSpec((1, D_MODEL, D_MODEL), lambda b, i: (layer, 0, 0)),
            vec(D_MODEL),
            slab(D_MODEL, D_FF),
            slab(D_FF, D_MODEL),
        ],
        out_specs=[
            pl.BlockSpec((1, TQ_ATT, D_MODEL), lambda b, i: (b, i, 0)),
            slab_out(D_MODEL, D_FF),
            slab_out(D_FF, D_MODEL),
        ],
        out_shape=[
            jax.ShapeDtypeStruct((BATCH, SEQ, D_MODEL), F32),
            jax.ShapeDtypeStruct((1, D_MODEL, D_FF), BF16),
            jax.ShapeDtypeStruct((1, D_FF, D_MODEL), BF16),
        ],
        scratch_shapes=[
            pltpu.VMEM((2 * N_KV_HEADS, WINDOW + TQ_ATT, LANES), BF16),
            pltpu.VMEM((2 * N_KV_HEADS, WINDOW + TQ_ATT, LANES), BF16),
            pltpu.VMEM((2 * N_Q_HEADS, 2 * WINDOW, WINDOW), F32),
            pltpu.VMEM((TQ_ATT, ATTN_WIDTH), F32),
            pltpu.VMEM((2 * 2 * N_KV_HEADS, 2 * WINDOW, 2 * WINDOW), F32),
            pltpu.VMEM((2 * N_Q_HEADS, 2 * WINDOW, WINDOW), BF16),
            pltpu.VMEM((TQ_ATT, ATTN_WIDTH), BF16),
            pltpu.VMEM((TQ_ATT, D_MODEL), F32),
            pltpu.VMEM((D_MODEL, D_MODEL), BF16),
        ],
        compiler_params=pltpu.CompilerParams(
            dimension_semantics=("arbitrary", "arbitrary"), vmem_limit_bytes=VMEM_LIMIT),
        name="attn_outproj",
    )(q, kv, kv, ssm_tm, x, mod, sinks, attn_g, w_out, post_g, w1, w2)


def _mlp_kernel(x_ref, mod_ref, g_ref, w1_ref, w2_ref, pg_ref, o_ref, acc_ref):
    x = x_ref[0]
    mod = mod_ref[0, 0]
    h = (_rms(x, g_ref[0] * (1.0 + mod[4:5])) + mod[3:4]).astype(BF16)
    for c in range(D_FF // FF_CHUNK):
        cols = slice(c * FF_CHUNK, (c + 1) * FF_CHUNK)
        f = jnp.dot(h, w1_ref[0, :, cols], preferred_element_type=F32)
        f = jnp.square(jnp.maximum(f, 0.0)).astype(BF16)
        part = jnp.dot(f, w2_ref[0, cols, :], preferred_element_type=F32)
        if c == 0:
            acc_ref[...] = part
        else:
            acc_ref[...] += part
    o_ref[0] = x + _rms(acc_ref[...], mod[5:6] * pg_ref[0])


def _mlp(layer, x, mod, pre_g, w1, w2, post_g):
    nt = SEQ // TM_MLP
    vec = lambda n: pl.BlockSpec((1, 1, n), lambda b, i: (layer, 0, 0))
    return pl.pallas_call(
        _mlp_kernel,
        grid=(BATCH, nt),
        in_specs=[
            pl.BlockSpec((1, TM_MLP, D_MODEL), lambda b, i: (b, i, 0)),
            pl.BlockSpec((1, 1, N_MOD, D_MODEL), lambda b, i: (layer, b, 0, 0)),
            vec(D_MODEL),
            pl.BlockSpec((1, D_MODEL, D_FF), lambda b, i: (0, 0, 0), pipeline_mode=pl.Buffered(1)),
            pl.BlockSpec((1, D_FF, D_MODEL), lambda b, i: (0, 0, 0), pipeline_mode=pl.Buffered(1)),
            vec(D_MODEL),
        ],
        out_specs=pl.BlockSpec((1, TM_MLP, D_MODEL), lambda b, i: (b, i, 0)),
        out_shape=jax.ShapeDtypeStruct((BATCH, SEQ, D_MODEL), F32),
        scratch_shapes=[pltpu.VMEM((TM_MLP, D_MODEL), F32)],
        compiler_params=pltpu.CompilerParams(
            dimension_semantics=("arbitrary", "arbitrary"), vmem_limit_bytes=VMEM_LIMIT),
        name="mlp",
    )(x, mod, pre_g, w1, w2, post_g)


def _ssm_params(lam_re, lam_im, log_dt, b_re, b_im, c_re, c_im):
    dt = jnp.exp(log_dt)[..., None]
    mag = jnp.exp(lam_re * dt)
    ang = lam_im * dt
    ab_r = mag * jnp.cos(ang)
    ab_i = mag * jnp.sin(ang)
    nr = ab_r - 1.0
    ni = ab_i
    den = lam_re * lam_re + lam_im * lam_im
    f_r = (nr * lam_re + ni * lam_im) / den
    f_i = (ni * lam_re - nr * lam_im) / den
    bb_r = f_r[..., None] * b_re - f_i[..., None] * b_im
    bb_i = f_r[..., None] * b_im + f_i[..., None] * b_re
    gpt = MXU_DIM // STATE
    gps = LANES // SSM_GROUP
    n_t = STATE_COLS // MXU_DIM
    shape_b = (n_t, LANES, MXU_DIM)
    n_i = lax.broadcasted_iota(jnp.int32, shape_b, 0)
    g_i = lax.broadcasted_iota(jnp.int32, shape_b, 1) // SSM_GROUP
    h_i = lax.broadcasted_iota(jnp.int32, shape_b, 2) // STATE
    mask_b = g_i == gpt * (n_i % (gps // gpt)) + h_i

    def b_tiles(bb):
        blk = bb.reshape(DEPTH, n_t, gpt, STATE, SSM_GROUP).transpose(0, 1, 4, 2, 3)
        blk = blk.reshape(DEPTH, n_t, 1, SSM_GROUP, MXU_DIM)
        rep = jnp.broadcast_to(blk, (DEPTH, n_t, gps, SSM_GROUP, MXU_DIM))
        return jnp.where(mask_b, rep.reshape(DEPTH, n_t, LANES, MXU_DIM), 0.0)

    wb = jnp.concatenate([b_tiles(bb_r), b_tiles(bb_i)], axis=1).astype(BF16)

    gpc = MXU_DIM // SSM_GROUP
    shape_c = (SCAN_CHUNKS, SCAN_COLS, MXU_DIM)
    k_i = lax.broadcasted_iota(jnp.int32, shape_c, 0)
    r_i = lax.broadcasted_iota(jnp.int32, shape_c, 1) // STATE
    c_i = lax.broadcasted_iota(jnp.int32, shape_c, 2) // SSM_GROUP
    mask_c = c_i == (gps * k_i + r_i) % gpc

    def c_rows(cc):
        blk = cc.transpose(0, 1, 3, 2).reshape(DEPTH, SCAN_CHUNKS, SCAN_COLS, 1, SSM_GROUP)
        rep = jnp.broadcast_to(blk, (DEPTH, SCAN_CHUNKS, SCAN_COLS, gpc, SSM_GROUP))
        return jnp.where(mask_c, rep.reshape(DEPTH, SCAN_CHUNKS, SCAN_COLS, MXU_DIM), 0.0)

    wc = jnp.concatenate([c_rows(c_re), c_rows(-c_im)], axis=2).astype(BF16)
    a_re = ab_r.reshape(DEPTH, 1, STATE_COLS)
    a_im = ab_i.reshape(DEPTH, 1, STATE_COLS)
    return wb, a_re, a_im, wc


def kernel(x, c, w_ada, b_ada, pre_mix_g, w_in, attn_sinks, lam_re, lam_im, log_dt, b_re, b_im,
           c_re, c_im, d_skip, w_glu, b_glu, attn_out_g, ssm_out_g, w_out, post_mix_g, pre_mlp_g,
           w_mlp_in, w_mlp_out, post_mlp_g):
    row = lambda a: a.reshape(DEPTH, 1, a.shape[-1])
    mod = _modulation(c, w_ada, b_ada).reshape(DEPTH, BATCH, N_MOD, D_MODEL)
    wb, a_re, a_im, wc = _ssm_params(lam_re, lam_im, log_dt, b_re, b_im, c_re, c_im)
    sinks = row(attn_sinks)
    for layer in range(DEPTH):
        q, kv, ssm_tm = _mix(layer, x, mod, row(pre_mix_g), w_in, wb, a_re, a_im, wc,
                             row(d_skip), w_glu, row(b_glu), row(ssm_out_g))
        x, w1_b, w2_b = _attn(layer, q, kv, ssm_tm, x, mod, sinks, row(attn_out_g), w_out,
                              row(post_mix_g), w_mlp_in, w_mlp_out)
        x = _mlp(layer, x, mod, row(pre_mlp_g), w1_b, w2_b, row(post_mlp_g))
    return x
```

```python
import functools

import jax
import jax.numpy as jnp
from jax import lax
from jax.experimental import pallas as pl
from jax.experimental.pallas import tpu as pltpu

D_MODEL = 1024
BATCH = 8
SEQ = 4096
DEPTH = 4
ATTN_WIDTH = 512
SSM_WIDTH = 512
HEAD_DIM = 64
N_Q_HEADS = 8
N_KV_HEADS = 2
Q_PER_KV = 4
KV_WIDTH = 128
WINDOW = 128
SSM_GROUP = 16
N_SSM_GROUPS = 32
STATE = 64
D_FF = 4096
IN_WIDTH = 1280
N_MOD = 6
EPS = 1e-6
NEG_INF = -1e30
LOG2E = 1.4426950408889634

LANES = 128
SUBLANES = 8
MXU_DIM = 256

STATE_COLS = N_SSM_GROUPS * STATE
BF16 = jnp.bfloat16
F32 = jnp.float32

VMEM_LIMIT = 56 * 1024 * 1024

TT_SSM = 64
SSM_SUB = 2
TQ_ATT = 1024
ATT_SUB = 2
TM_MLP = 1024
FF_CHUNK = 1024
SCAN_COLS = LANES * STATE // SSM_GROUP
SCAN_CHUNKS = STATE_COLS // SCAN_COLS


def _rms(x, g):
    return x * lax.rsqrt(jnp.mean(x * x, axis=-1, keepdims=True) + EPS) * g


def _dot_f32_lhs(a, w):
    return lax.dot_general(a, w, (((1,), (0,)), ((), ())), preferred_element_type=F32)


def _mod_kernel(c_ref, w_ref, b_ref, o_ref):
    c = c_ref[...]
    ca = (c * jax.nn.sigmoid(c)).astype(BF16)
    o_ref[0] = jnp.dot(ca, w_ref[0].astype(BF16), preferred_element_type=F32) + b_ref[0]


def _modulation(c, w_ada, b_ada):
    nb = 1536
    return pl.pallas_call(
        _mod_kernel,
        grid=(DEPTH, N_MOD * D_MODEL // nb),
        in_specs=[
            pl.BlockSpec((BATCH, D_MODEL), lambda l, j: (0, 0)),
            pl.BlockSpec((1, D_MODEL, nb), lambda l, j: (l, 0, j)),
            pl.BlockSpec((1, 1, nb), lambda l, j: (l, 0, j)),
        ],
        out_specs=pl.BlockSpec((1, BATCH, nb), lambda l, j: (l, 0, j)),
        out_shape=jax.ShapeDtypeStruct((DEPTH, BATCH, N_MOD * D_MODEL), F32),
        compiler_params=pltpu.CompilerParams(
            dimension_semantics=("arbitrary", "arbitrary"), vmem_limit_bytes=VMEM_LIMIT),
        name="adaln_mod",
    )(c, w_ada, b_ada.reshape(DEPTH, 1, N_MOD * D_MODEL))


TT_MIX = SSM_SUB * TT_SSM
PAIRS = TT_SSM // 2
PROWS = PAIRS * BATCH


def _mix_kernel(x_ref, mod_ref, pg_ref, win_ref, wb_ref, are_ref, aim_ref, wc_ref, wca_ref, wf_ref,
                d_ref, wglu_ref, bglu_ref, g_ref, q_ref, kv_ref, o_ref,
                hn_ref, winb_ref, uall_ref, utb_ref, otb_ref, ost_ref, state_ref, wglub_ref,
                *hs_refs):
    @pl.when(pl.program_id(0) == 0)
    def _():
        state_ref[...] = jnp.zeros_like(state_ref)
        wglub_ref[...] = wglu_ref[0].astype(BF16)
        winb_ref[...] = win_ref[0].astype(BF16)

    n_slab = SSM_WIDTH // LANES
    tiles_per_half = SCAN_COLS // MXU_DIM
    per = SCAN_CHUNKS // (SSM_WIDTH // MXU_DIM)
    u_col = ATTN_WIDTH + 2 * KV_WIDTH

    half = BATCH // 2
    for part in range(2):
        for b in range(part * half, (part + 1) * half):
            mod = mod_ref[0, b]
            hn_ref[b * TT_MIX:(b + 1) * TT_MIX, :] = (
                _rms(x_ref[b], pg_ref[0] * (1.0 + mod[1:2])) + mod[0:1]).astype(BF16)
        rows = slice(part * half * TT_MIX, (part + 1) * half * TT_MIX)
        u = jnp.dot(hn_ref[rows, :], winb_ref[:, u_col:], preferred_element_type=F32)
        for s in range(n_slab):
            uall_ref[s, rows, :] = u[:, s * LANES:(s + 1) * LANES]

    for sub in range(SSM_SUB):
        for b in range(BATCH):
            for par in range(2):
                r0 = b * TT_MIX + sub * TT_SSM + par
                for s in range(n_slab):
                    utb_ref[sub * n_slab + s, pl.ds(par * PROWS + b, PAIRS, stride=BATCH), :] = (
                        uall_ref[s, pl.ds(r0, PAIRS, stride=2), :])

    def qkv_piece(c):
        if c == 0:
            q = jnp.dot(hn_ref[...], winb_ref[:, :ATTN_WIDTH], preferred_element_type=F32)
            q_ref[...] = (q * (HEAD_DIM ** -0.5 * LOG2E)).astype(BF16).reshape(
                BATCH, TT_MIX, ATTN_WIDTH)
        else:
            kv = jnp.dot(hn_ref[...], winb_ref[:, ATTN_WIDTH:u_col], preferred_element_type=F32)
            k = kv[:, :KV_WIDTH]
            v = kv[:, KV_WIDTH:]
            shape = (BATCH, TT_MIX, KV_WIDTH)
            kv_ref[:, :, 0 * KV_WIDTH:1 * KV_WIDTH] = k.astype(BF16).reshape(shape)
            kv_ref[:, :, 1 * KV_WIDTH:2 * KV_WIDTH] = v.astype(BF16).reshape(shape)
            kv_ref[:, :, 2 * KV_WIDTH:3 * KV_WIDTH] = (
                pltpu.roll(k, HEAD_DIM, 1).astype(BF16).reshape(shape))
            kv_ref[:, :, 3 * KV_WIDTH:4 * KV_WIDTH] = (
                pltpu.roll(v, HEAD_DIM, 1).astype(BF16).reshape(shape))


    def project_in(sub, k):
        ub = utb_ref[sub * n_slab + k]
        pair = jnp.concatenate([ub[:PROWS], ub[PROWS:]], axis=1)
        hs = hs_refs[sub * SCAN_CHUNKS + k]
        for part in range(2):
            for j in range(tiles_per_half):
                n = part * (STATE_COLS // MXU_DIM) + k * tiles_per_half + j
                c0 = part * SCAN_COLS + j * MXU_DIM
                hs[SUBLANES:, c0:c0 + MXU_DIM] = _dot_f32_lhs(pair, wb_ref[0, n])

    def scan(sub, k):
        hs = hs_refs[sub * SCAN_CHUNKS + k]
        cols = slice(k * SCAN_COLS, (k + 1) * SCAN_COLS)
        ar = jnp.broadcast_to(are_ref[0, :, cols], (BATCH, SCAN_COLS))
        ai = jnp.broadcast_to(aim_ref[0, :, cols], (BATCH, SCAN_COLS))
        hr = state_ref[k, :, :SCAN_COLS]
        hi = state_ref[k, :, SCAN_COLS:]
        hs[:SUBLANES, :SCAN_COLS] = hr
        hs[:SUBLANES, SCAN_COLS:] = hi
        for t in range(PAIRS):
            rows = slice(SUBLANES + t * BATCH, SUBLANES + (t + 1) * BATCH)
            nr = ar * hr - ai * hi + hs[rows, :SCAN_COLS]
            ni = ar * hi + ai * hr + hs[rows, SCAN_COLS:]
            hs[rows, :SCAN_COLS] = nr
            hs[rows, SCAN_COLS:] = ni
            hr, hi = nr, ni
        state_ref[k, :, :SCAN_COLS] = hr
        state_ref[k, :, SCAN_COLS:] = hi

    def project_out(sub, n):
        chunks = range(n * per, (n + 1) * per)
        bufs = [hs_refs[sub * SCAN_CHUNKS + k] for k in chunks]
        h_prev = jnp.concatenate([hs[:PROWS, :] for hs in bufs], axis=1)
        h_odd = jnp.concatenate([hs[SUBLANES:, :] for hs in bufs], axis=1)
        u_even = jnp.concatenate([utb_ref[sub * n_slab + k][:PROWS] for k in chunks], axis=1)
        w_odd = wc_ref[0, n * per:(n + 1) * per].reshape(per * 2 * SCAN_COLS, MXU_DIM)
        w_even = wca_ref[0, n * per:(n + 1) * per].reshape(per * 2 * SCAN_COLS, MXU_DIM)
        y_even = _dot_f32_lhs(h_prev, w_even) + _dot_f32_lhs(u_even, wf_ref[0, n])
        return jnp.concatenate([y_even, _dot_f32_lhs(h_odd, w_odd)], axis=0)

    def finish(sub, ys):
        u = jnp.concatenate([utb_ref[sub * n_slab + s] for s in range(n_slab)], axis=1)
        y = jnp.concatenate(ys, axis=1) + d_ref[0] * u
        z = jax.nn.gelu(y)
        gate = jax.nn.sigmoid(
            jnp.dot(z.astype(BF16), wglub_ref[...], preferred_element_type=F32) + bglu_ref[0])
        res = _rms(z * gate, g_ref[0])
        for s in range(n_slab):
            otb_ref[sub * n_slab + s] = res[:, s * LANES:(s + 1) * LANES]
        t0 = sub * TT_SSM
        for b in range(BATCH):
            for s in range(n_slab):
                for par in range(2):
                    ost_ref[b * n_slab + s, pl.ds(t0 + par, PAIRS, stride=2), :] = (
                        otb_ref[sub * n_slab + s, pl.ds(par * PROWS + b, PAIRS, stride=BATCH), :])
                c0 = b * SSM_WIDTH + s * LANES
                o_ref[t0:t0 + TT_SSM, c0:c0 + LANES] = (
                    ost_ref[b * n_slab + s, t0:t0 + TT_SSM, :].astype(BF16))

    pending = None
    qkv = [0, 1]
    for sub in range(SSM_SUB):
        project_in(sub, 0)
        for k in range(SCAN_CHUNKS):
            if k + 1 < SCAN_CHUNKS:
                project_in(sub, k + 1)
            if pending is not None and k < len(pending[1]):
                pending[2].append(project_out(pending[0], pending[1][k]))
            elif qkv and k % 2 == 1:
                qkv_piece(qkv.pop(0))
            scan(sub, k)
        if pending is not None:
            finish(pending[0], pending[2])
        pending = (sub, list(range(SSM_WIDTH // MXU_DIM)), [])
    finish(pending[0], [project_out(pending[0], n) for n in pending[1]])
    for c in qkv:
        qkv_piece(c)


def _mix(layer, x, mod, pre_g, w_in, wb, a_re, a_im, wc, wca, wf, d_skip, w_glu, b_glu, ssm_g):
    rows = TT_SSM * BATCH
    vec = lambda n: pl.BlockSpec((1, 1, n), lambda i: (layer, 0, 0))
    return pl.pallas_call(
        _mix_kernel,
        grid=(SEQ // TT_MIX,),
        in_specs=[
            pl.BlockSpec((BATCH, TT_MIX, D_MODEL), lambda i: (0, i, 0)),
            pl.BlockSpec((1, BATCH, N_MOD, D_MODEL), lambda i: (layer, 0, 0, 0)),
            vec(D_MODEL),
            pl.BlockSpec((1, D_MODEL, IN_WIDTH), lambda i: (layer, 0, 0)),
            pl.BlockSpec((1,) + wb.shape[1:], lambda i: (layer, 0, 0, 0)),
            vec(STATE_COLS), vec(STATE_COLS),
            pl.BlockSpec((1,) + wc.shape[1:], lambda i: (layer, 0, 0, 0)),
            pl.BlockSpec((1,) + wca.shape[1:], lambda i: (layer, 0, 0, 0)),
            pl.BlockSpec((1,) + wf.shape[1:], lambda i: (layer, 0, 0, 0)),
            vec(SSM_WIDTH),
            pl.BlockSpec((1, SSM_WIDTH, SSM_WIDTH), lambda i: (layer, 0, 0)),
            vec(SSM_WIDTH), vec(SSM_WIDTH),
        ],
        out_specs=[
            pl.BlockSpec((BATCH, TT_MIX, ATTN_WIDTH), lambda i: (0, i, 0)),
            pl.BlockSpec((BATCH, TT_MIX, 4 * KV_WIDTH), lambda i: (0, i, 0)),
            pl.BlockSpec((TT_MIX, BATCH * SSM_WIDTH), lambda i: (i, 0)),
        ],
        out_shape=[
            jax.ShapeDtypeStruct((BATCH, SEQ, ATTN_WIDTH), BF16),
            jax.ShapeDtypeStruct((BATCH, SEQ, 4 * KV_WIDTH), BF16),
            jax.ShapeDtypeStruct((SEQ, BATCH * SSM_WIDTH), BF16),
        ],
        scratch_shapes=[
            pltpu.VMEM((BATCH * TT_MIX, D_MODEL), BF16),
            pltpu.VMEM((D_MODEL, IN_WIDTH), BF16),
            pltpu.VMEM((SSM_WIDTH // LANES, BATCH * TT_MIX, LANES), F32),
            pltpu.VMEM((SSM_SUB * SSM_WIDTH // LANES, rows, LANES), F32),
            pltpu.VMEM((SSM_SUB * SSM_WIDTH // LANES, rows, LANES), F32),
            pltpu.VMEM((BATCH * SSM_WIDTH // LANES, TT_MIX, LANES), F32),
            pltpu.VMEM((SCAN_CHUNKS, BATCH, 2 * SCAN_COLS), F32),
            pltpu.VMEM((SSM_WIDTH, SSM_WIDTH), BF16),
        ] + [pltpu.VMEM((SUBLANES + PROWS, 2 * SCAN_COLS), F32)
             for _ in range(SSM_SUB * SCAN_CHUNKS)],
        compiler_params=pltpu.CompilerParams(
            dimension_semantics=("arbitrary",), vmem_limit_bytes=VMEM_LIMIT),
        name="mix_in_s5",
    )(x, mod, pre_g, w_in, wb, a_re, a_im, wc, wca, wf, d_skip, w_glu, b_glu, ssm_g)


def _attn_kernel(q_ref, kvc_ref, kvp_ref, ssm_ref, x_ref, mod_ref, sink_ref, ag_ref, wout_ref,
                 pg_ref, w1_ref, w2_ref, o_ref, w1b_ref, w2b_ref, kpad_ref, vpad_ref, bias_ref,
                 heads_ref, st_ref, p_ref, attn_ref, mixed_ref, woutb_ref):
    i = pl.program_id(1)
    nq = TQ_ATT // WINDOW

    w1b_ref[0] = w1_ref[0].astype(BF16)
    w2b_ref[0] = w2_ref[0].astype(BF16)

    @pl.when((pl.program_id(0) == 0) & (i == 0))
    def _():
        j = lax.broadcasted_iota(jnp.int32, (2 * WINDOW, WINDOW), 0)
        r = lax.broadcasted_iota(jnp.int32, (2 * WINDOW, WINDOW), 1)
        diff = WINDOW + r - j
        valid = (diff >= 0) & (diff < WINDOW)
        for h in range(N_Q_HEADS):
            slope = 2.0 ** (-8.0 * (h + 1) / N_Q_HEADS)
            bias = -(slope * LOG2E) * diff.astype(F32)
            bias_ref[h] = jnp.where(valid, bias, NEG_INF)
            bias_ref[N_Q_HEADS + h] = jnp.where(valid & (j >= WINDOW), bias, NEG_INF)
        woutb_ref[...] = wout_ref[0].astype(BF16)

    lane = lax.broadcasted_iota(jnp.int32, (1, LANES), 1)
    for src, rows in ((kvp_ref, slice(0, WINDOW)), (kvc_ref, slice(WINDOW, WINDOW + TQ_ATT))):
        for hk in range(N_KV_HEADS):
            for par in range(2):
                keep = (lane >= par * HEAD_DIM) & (lane < (par + 1) * HEAD_DIM)
                off = 0 if par == hk else 2 * KV_WIDTH
                kpad_ref[hk * 2 + par, rows, :] = jnp.where(
                    keep, src[0, :, off:off + KV_WIDTH], jnp.zeros((), BF16))
                vpad_ref[hk * 2 + par, rows, :] = jnp.where(
                    keep, src[0, :, off + KV_WIDTH:off + 2 * KV_WIDTH], jnp.zeros((), BF16))

    nt = (((1,), (1,)), ((), ()))
    tn = (((0,), (0,)), ((), ()))

    def scores(jb):
        q0 = jb * WINDOW
        for hk in range(N_KV_HEADS):
            qq = jnp.concatenate(
                [q_ref[0, q0:q0 + WINDOW, (2 * hk + l2) * LANES:(2 * hk + l2 + 1) * LANES]
                 for l2 in range(2)], axis=0)
            for par in range(2):
                kp = kpad_ref[hk * 2 + par, q0:q0 + 2 * WINDOW, :]
                st_ref[(jb % 2) * 2 * N_KV_HEADS + hk * 2 + par] = lax.dot_general(
                    kp, qq, nt, preferred_element_type=F32)

    def softmax(jb):
        table = jnp.where(i == 0, N_Q_HEADS, 0) if jb == 0 else 0
        for hk in range(N_KV_HEADS):
            for par in range(2):
                for l2 in range(2):
                    h = Q_PER_KV * hk + 2 * l2 + par
                    s = (st_ref[(jb % 2) * 2 * N_KV_HEADS + hk * 2 + par, :,
                                l2 * WINDOW:(l2 + 1) * WINDOW] + bias_ref[table + h])
                    sink = sink_ref[0, 0, h] * LOG2E
                    m = jnp.maximum(jnp.max(s, axis=0, keepdims=True), sink)
                    e = jnp.exp2(s - m)
                    denom = jnp.sum(e, axis=0, keepdims=True) + jnp.exp2(sink - m)
                    p_ref[(jb % 2) * N_Q_HEADS + h] = (e * (1.0 / denom)).astype(BF16)

    def values(jb):
        q0 = jb * WINDOW
        for lt in range(N_Q_HEADS // 2):
            hk = 2 * lt // Q_PER_KV
            acc = None
            for par in range(2):
                vp = vpad_ref[hk * 2 + par, q0:q0 + 2 * WINDOW, :]
                o = lax.dot_general(p_ref[(jb % 2) * N_Q_HEADS + 2 * lt + par], vp, tn,
                                    preferred_element_type=F32)
                acc = o if acc is None else acc + o
            heads_ref[q0:q0 + WINDOW, lt * LANES:(lt + 1) * LANES] = acc

    mod = mod_ref[0, 0]
    gate = mod[2:3] * pg_ref[0]

    def out_proj_pieces(part):
        rows = slice(part * (TQ_ATT // ATT_SUB), (part + 1) * (TQ_ATT // ATT_SUB))
        n_col = D_MODEL // MXU_DIM

        def piece(c):
            if c == 0:
                attn_ref[rows, :] = _rms(heads_ref[rows, :], ag_ref[0]).astype(BF16)
            cols = slice(c * MXU_DIM, (c + 1) * MXU_DIM)
            heads = jnp.concatenate([attn_ref[rows, :], ssm_ref[rows, :]], axis=1)
            mixed_ref[rows, cols] = jnp.dot(heads, woutb_ref[:, cols], preferred_element_type=F32)
            if c == n_col - 1:
                o_ref[0, rows, :] = x_ref[0, rows, :] + _rms(mixed_ref[rows, :], gate)

        return [functools.partial(piece, c) for c in range(n_col)]

    per_part = nq // ATT_SUB
    queue = []
    scores(0)
    for jb in range(nq):
        if jb + 1 < nq:
            scores(jb + 1)
        softmax(jb)
        values(jb)
        for _ in range(-(-len(queue) // (per_part - jb % per_part))):
            queue.pop(0)()
        if (jb + 1) % per_part == 0:
            queue.extend(out_proj_pieces(jb // per_part))
    for piece in queue:
        piece()


def _attn(layer, q, kv, ssm_tm, x, mod, sinks, attn_g, w_out, post_g, w1, w2):
    nt = SEQ // TQ_ATT
    per_tile = TQ_ATT // WINDOW
    steps = BATCH * nt
    vec = lambda n: pl.BlockSpec((1, 1, n), lambda b, i: (layer, 0, 0))
    slab = lambda rows, cols: pl.BlockSpec(
        (1, rows // steps, cols), lambda b, i: (layer, b * nt + i, 0))
    slab_out = lambda rows, cols: pl.BlockSpec(
        (1, rows // steps, cols), lambda b, i: (0, b * nt + i, 0))
    return pl.pallas_call(
        _attn_kernel,
        grid=(BATCH, nt),
        in_specs=[
            pl.BlockSpec((1, TQ_ATT, ATTN_WIDTH), lambda b, i: (b, i, 0)),
            pl.BlockSpec((1, TQ_ATT, 4 * KV_WIDTH), lambda b, i: (b, i, 0)),
            pl.BlockSpec((1, WINDOW, 4 * KV_WIDTH),
                         lambda b, i: (b, jnp.maximum(i * per_tile - 1, 0), 0)),
            pl.BlockSpec((TQ_ATT, SSM_WIDTH), lambda b, i: (i, b)),
            pl.BlockSpec((1, TQ_ATT, D_MODEL), lambda b, i: (b, i, 0)),
            pl.BlockSpec((1, 1, N_MOD, D_MODEL), lambda b, i: (layer, b, 0, 0)),
            pl.BlockSpec((1, 1, N_Q_HEADS), lambda b, i: (layer, 0, 0), memory_space=pltpu.SMEM),
            vec(ATTN_WIDTH),
            pl.BlockSpec((1, D_MODEL, D_MODEL), lambda b, i: (layer, 0, 0)),
            vec(D_MODEL),
            slab(D_MODEL, D_FF),
            slab(D_FF, D_MODEL),
        ],
        out_specs=[
            pl.BlockSpec((1, TQ_ATT, D_MODEL), lambda b, i: (b, i, 0)),
            slab_out(D_MODEL, D_FF),
            slab_out(D_FF, D_MODEL),
        ],
        out_shape=[
            jax.ShapeDtypeStruct((BATCH, SEQ, D_MODEL), F32),
            jax.ShapeDtypeStruct((1, D_MODEL, D_FF), BF16),
            jax.ShapeDtypeStruct((1, D_FF, D_MODEL), BF16),
        ],
        scratch_shapes=[
            pltpu.VMEM((2 * N_KV_HEADS, WINDOW + TQ_ATT, LANES), BF16),
            pltpu.VMEM((2 * N_KV_HEADS, WINDOW + TQ_ATT, LANES), BF16),
            pltpu.VMEM((2 * N_Q_HEADS, 2 * WINDOW, WINDOW), F32),
            pltpu.VMEM((TQ_ATT, ATTN_WIDTH), F32),
            pltpu.VMEM((2 * 2 * N_KV_HEADS, 2 * WINDOW, 2 * WINDOW), F32),
            pltpu.VMEM((2 * N_Q_HEADS, 2 * WINDOW, WINDOW), BF16),
            pltpu.VMEM((TQ_ATT, ATTN_WIDTH), BF16),
            pltpu.VMEM((TQ_ATT, D_MODEL), F32),
            pltpu.VMEM((D_MODEL, D_MODEL), BF16),
        ],
        compiler_params=pltpu.CompilerParams(
            dimension_semantics=("arbitrary", "arbitrary"), vmem_limit_bytes=VMEM_LIMIT),
        name="attn_outproj",
    )(q, kv, kv, ssm_tm, x, mod, sinks, attn_g, w_out, post_g, w1, w2)


def _mlp_kernel(x_ref, mod_ref, g_ref, w1_ref, w2_ref, pg_ref, o_ref, acc_ref):
    x = x_ref[0]
    mod = mod_ref[0, 0]
    h = (_rms(x, g_ref[0] * (1.0 + mod[4:5])) + mod[3:4]).astype(BF16)
    for c in range(D_FF // FF_CHUNK):
        cols = slice(c * FF_CHUNK, (c + 1) * FF_CHUNK)
        f = jnp.dot(h, w1_ref[0, :, cols], preferred_element_type=F32)
        f = jnp.square(jnp.maximum(f, 0.0)).astype(BF16)
        part = jnp.dot(f, w2_ref[0, cols, :], preferred_element_type=F32)
        if c == 0:
            acc_ref[...] = part
        else:
            acc_ref[...] += part
    o_ref[0] = x + _rms(acc_ref[...], mod[5:6] * pg_ref[0])


def _mlp(layer, x, mod, pre_g, w1, w2, post_g):
    nt = SEQ // TM_MLP
    vec = lambda n: pl.BlockSpec((1, 1, n), lambda b, i: (layer, 0, 0))
    return pl.pallas_call(
        _mlp_kernel,
        grid=(BATCH, nt),
        in_specs=[
            pl.BlockSpec((1, TM_MLP, D_MODEL), lambda b, i: (b, i, 0)),
            pl.BlockSpec((1, 1, N_MOD, D_MODEL), lambda b, i: (layer, b, 0, 0)),
            vec(D_MODEL),
            pl.BlockSpec((1, D_MODEL, D_FF), lambda b, i: (0, 0, 0), pipeline_mode=pl.Buffered(1)),
            pl.BlockSpec((1, D_FF, D_MODEL), lambda b, i: (0, 0, 0), pipeline_mode=pl.Buffered(1)),
            vec(D_MODEL),
        ],
        out_specs=pl.BlockSpec((1, TM_MLP, D_MODEL), lambda b, i: (b, i, 0)),
        out_shape=jax.ShapeDtypeStruct((BATCH, SEQ, D_MODEL), F32),
        scratch_shapes=[pltpu.VMEM((TM_MLP, D_MODEL), F32)],
        compiler_params=pltpu.CompilerParams(
            dimension_semantics=("arbitrary", "arbitrary"), vmem_limit_bytes=VMEM_LIMIT),
        name="mlp",
    )(x, mod, pre_g, w1, w2, post_g)


def _ssm_params(lam_re, lam_im, log_dt, b_re, b_im, c_re, c_im):
    dt = jnp.exp(log_dt)[..., None]
    mag = jnp.exp(lam_re * dt)
    ang = lam_im * dt
    ab_r = mag * jnp.cos(ang)
    ab_i = mag * jnp.sin(ang)
    nr = ab_r - 1.0
    ni = ab_i
    den = lam_re * lam_re + lam_im * lam_im
    f_r = (nr * lam_re + ni * lam_im) / den
    f_i = (ni * lam_re - nr * lam_im) / den
    bb_r = f_r[..., None] * b_re - f_i[..., None] * b_im
    bb_i = f_r[..., None] * b_im + f_i[..., None] * b_re
    gpt = MXU_DIM // STATE
    gps = LANES // SSM_GROUP
    n_t = STATE_COLS // MXU_DIM
    shape_b = (n_t, LANES, MXU_DIM)
    n_i = lax.broadcasted_iota(jnp.int32, shape_b, 0)
    g_i = lax.broadcasted_iota(jnp.int32, shape_b, 1) // SSM_GROUP
    h_i = lax.broadcasted_iota(jnp.int32, shape_b, 2) // STATE
    mask_b = g_i == gpt * (n_i % (gps // gpt)) + h_i

    def b_tiles(bb):
        blk = bb.reshape(DEPTH, n_t, gpt, STATE, SSM_GROUP).transpose(0, 1, 4, 2, 3)
        blk = blk.reshape(DEPTH, n_t, 1, SSM_GROUP, MXU_DIM)
        rep = jnp.broadcast_to(blk, (DEPTH, n_t, gps, SSM_GROUP, MXU_DIM))
        return jnp.where(mask_b, rep.reshape(DEPTH, n_t, LANES, MXU_DIM), 0.0)

    ba_r = ab_r[..., None] * bb_r - ab_i[..., None] * bb_i
    ba_i = ab_r[..., None] * bb_i + ab_i[..., None] * bb_r
    pair_tiles = lambda first, second: jnp.concatenate([b_tiles(first), b_tiles(second)], axis=2)
    wb = jnp.concatenate([pair_tiles(ba_r, bb_r), pair_tiles(ba_i, bb_i)],
                         axis=1).astype(BF16)

    gpc = MXU_DIM // SSM_GROUP
    shape_c = (SCAN_CHUNKS, SCAN_COLS, MXU_DIM)
    k_i = lax.broadcasted_iota(jnp.int32, shape_c, 0)
    r_i = lax.broadcasted_iota(jnp.int32, shape_c, 1) // STATE
    c_i = lax.broadcasted_iota(jnp.int32, shape_c, 2) // SSM_GROUP
    mask_c = c_i == (gps * k_i + r_i) % gpc

    def c_rows(cc):
        blk = cc.transpose(0, 1, 3, 2).reshape(DEPTH, SCAN_CHUNKS, SCAN_COLS, 1, SSM_GROUP)
        rep = jnp.broadcast_to(blk, (DEPTH, SCAN_CHUNKS, SCAN_COLS, gpc, SSM_GROUP))
        return jnp.where(mask_c, rep.reshape(DEPTH, SCAN_CHUNKS, SCAN_COLS, MXU_DIM), 0.0)

    wc = jnp.concatenate([c_rows(c_re), c_rows(-c_im)], axis=2).astype(BF16)
    ca_r = c_re * ab_r[:, :, None, :] - c_im * ab_i[:, :, None, :]
    ca_i = c_re * ab_i[:, :, None, :] + c_im * ab_r[:, :, None, :]
    wca = jnp.concatenate([c_rows(ca_r), c_rows(-ca_i)], axis=2).astype(BF16)
    hp = lax.Precision.HIGHEST
    feed = (jnp.einsum('lgcp,lgpd->lgdc', c_re, bb_r, precision=hp)
            - jnp.einsum('lgcp,lgpd->lgdc', c_im, bb_i, precision=hp))
    n_out = SSM_WIDTH // MXU_DIM
    blk = feed.reshape(DEPTH, n_out, gpc, SSM_GROUP, 1, SSM_GROUP)
    rep = jnp.broadcast_to(blk, (DEPTH, n_out, gpc, SSM_GROUP, gpc, SSM_GROUP))
    r_g = lax.broadcasted_iota(jnp.int32, (MXU_DIM, MXU_DIM), 0) // SSM_GROUP
    c_g = lax.broadcasted_iota(jnp.int32, (MXU_DIM, MXU_DIM), 1) // SSM_GROUP
    wf = jnp.where(r_g == c_g, rep.reshape(DEPTH, n_out, MXU_DIM, MXU_DIM), 0.0).astype(BF16)
    a2_r = (ab_r * ab_r - ab_i * ab_i).reshape(DEPTH, 1, STATE_COLS)
    a2_i = (2.0 * ab_r * ab_i).reshape(DEPTH, 1, STATE_COLS)
    return wb, a2_r, a2_i, wc, wca, wf


def kernel(x, c, w_ada, b_ada, pre_mix_g, w_in, attn_sinks, lam_re, lam_im, log_dt, b_re, b_im,
           c_re, c_im, d_skip, w_glu, b_glu, attn_out_g, ssm_out_g, w_out, post_mix_g, pre_mlp_g,
           w_mlp_in, w_mlp_out, post_mlp_g):
    row = lambda a: a.reshape(DEPTH, 1, a.shape[-1])
    mod = _modulation(c, w_ada, b_ada).reshape(DEPTH, BATCH, N_MOD, D_MODEL)
    wb, a_re, a_im, wc, wca, wf = _ssm_params(lam_re, lam_im, log_dt, b_re, b_im, c_re, c_im)
    sinks = row(attn_sinks)
    for layer in range(DEPTH):
        q, kv, ssm_tm = _mix(layer, x, mod, row(pre_mix_g), w_in, wb, a_re, a_im, wc, wca, wf,
                             row(d_skip), w_glu, row(b_glu), row(ssm_out_g))
        x, w1_b, w2_b = _attn(layer, q, kv, ssm_tm, x, mod, sinks, row(attn_out_g), w_out,
                              row(post_mix_g), w_mlp_in, w_mlp_out)
        x = _mlp(layer, x, mod, row(pre_mlp_g), w1_b, w2_b, row(post_mlp_g))
    return x
```

```python
import functools

import jax
import jax.numpy as jnp
from jax import lax
from jax.experimental import pallas as pl
from jax.experimental.pallas import tpu as pltpu

D_MODEL = 1024
BATCH = 8
SEQ = 4096
DEPTH = 4
ATTN_WIDTH = 512
SSM_WIDTH = 512
HEAD_DIM = 64
N_Q_HEADS = 8
N_KV_HEADS = 2
Q_PER_KV = 4
KV_WIDTH = 128
WINDOW = 128
SSM_GROUP = 16
N_SSM_GROUPS = 32
STATE = 64
D_FF = 4096
IN_WIDTH = 1280
N_MOD = 6
EPS = 1e-6
NEG_INF = -1e30
LOG2E = 1.4426950408889634

LANES = 128
SUBLANES = 8
MXU_DIM = 256

STATE_COLS = N_SSM_GROUPS * STATE
BF16 = jnp.bfloat16
F32 = jnp.float32

VMEM_LIMIT = 56 * 1024 * 1024

TT_SSM = 64
SSM_SUB = 2
TQ_ATT = 1024
ATT_SUB = 2
SOFTMAX_ROWS = 64
TM_MLP = 1024
FF_CHUNK = 1024
SCAN_COLS = LANES * STATE // SSM_GROUP
SCAN_CHUNKS = STATE_COLS // SCAN_COLS


def _rms(x, g):
    return x * lax.rsqrt(jnp.mean(x * x, axis=-1, keepdims=True) + EPS) * g


def _dot_f32_lhs(a, w):
    return lax.dot_general(a, w, (((1,), (0,)), ((), ())), preferred_element_type=F32)


def _mod_kernel(c_ref, w_ref, b_ref, o_ref):
    c = c_ref[...]
    ca = (c * jax.nn.sigmoid(c)).astype(BF16)
    o_ref[0] = jnp.dot(ca, w_ref[0].astype(BF16), preferred_element_type=F32) + b_ref[0]


def _modulation(c, w_ada, b_ada):
    nb = 1536
    return pl.pallas_call(
        _mod_kernel,
        grid=(DEPTH, N_MOD * D_MODEL // nb),
        in_specs=[
            pl.BlockSpec((BATCH, D_MODEL), lambda l, j: (0, 0)),
            pl.BlockSpec((1, D_MODEL, nb), lambda l, j: (l, 0, j)),
            pl.BlockSpec((1, 1, nb), lambda l, j: (l, 0, j)),
        ],
        out_specs=pl.BlockSpec((1, BATCH, nb), lambda l, j: (l, 0, j)),
        out_shape=jax.ShapeDtypeStruct((DEPTH, BATCH, N_MOD * D_MODEL), F32),
        compiler_params=pltpu.CompilerParams(
            dimension_semantics=("arbitrary", "arbitrary"), vmem_limit_bytes=VMEM_LIMIT),
        name="adaln_mod",
    )(c, w_ada, b_ada.reshape(DEPTH, 1, N_MOD * D_MODEL))


TT_MIX = SSM_SUB * TT_SSM
PAIRS = TT_SSM // 2
PROWS = PAIRS * BATCH


def _mix_kernel(x_ref, mod_ref, pg_ref, win_ref, wb_ref, are_ref, aim_ref, wc_ref, wca_ref, wf_ref,
                d_ref, wglu_ref, bglu_ref, g_ref, q_ref, kv_ref, o_ref,
                hn_ref, winb_ref, uall_ref, utb_ref, otb_ref, ost_ref, state_ref, wglub_ref,
                *hs_refs):
    @pl.when(pl.program_id(0) == 0)
    def _():
        state_ref[...] = jnp.zeros_like(state_ref)
        wglub_ref[...] = wglu_ref[0].astype(BF16)
        winb_ref[...] = win_ref[0].astype(BF16)

    n_slab = SSM_WIDTH // LANES
    tiles_per_half = SCAN_COLS // MXU_DIM
    per = SCAN_CHUNKS // (SSM_WIDTH // MXU_DIM)
    u_col = ATTN_WIDTH + 2 * KV_WIDTH

    half = BATCH // 2
    for part in range(2):
        for b in range(part * half, (part + 1) * half):
            mod = mod_ref[0, b]
            hn_ref[b * TT_MIX:(b + 1) * TT_MIX, :] = (
                _rms(x_ref[b], pg_ref[0] * (1.0 + mod[1:2])) + mod[0:1]).astype(BF16)
        rows = slice(part * half * TT_MIX, (part + 1) * half * TT_MIX)
        u = jnp.dot(hn_ref[rows, :], winb_ref[:, u_col:], preferred_element_type=F32)
        for s in range(n_slab):
            uall_ref[s, rows, :] = u[:, s * LANES:(s + 1) * LANES]

    for sub in range(SSM_SUB):
        for b in range(BATCH):
            for par in range(2):
                r0 = b * TT_MIX + sub * TT_SSM + par
                for s in range(n_slab):
                    utb_ref[sub * n_slab + s, pl.ds(par * PROWS + b, PAIRS, stride=BATCH), :] = (
                        uall_ref[s, pl.ds(r0, PAIRS, stride=2), :])

    def qkv_piece(c):
        if c == 0:
            q = jnp.dot(hn_ref[...], winb_ref[:, :ATTN_WIDTH], preferred_element_type=F32)
            q_ref[...] = (q * (HEAD_DIM ** -0.5 * LOG2E)).astype(BF16).reshape(
                BATCH, TT_MIX, ATTN_WIDTH)
        else:
            kv = jnp.dot(hn_ref[...], winb_ref[:, ATTN_WIDTH:u_col], preferred_element_type=F32)
            k = kv[:, :KV_WIDTH]
            v = kv[:, KV_WIDTH:]
            shape = (BATCH, TT_MIX, KV_WIDTH)
            kv_ref[:, :, 0 * KV_WIDTH:1 * KV_WIDTH] = k.astype(BF16).reshape(shape)
            kv_ref[:, :, 1 * KV_WIDTH:2 * KV_WIDTH] = v.astype(BF16).reshape(shape)
            kv_ref[:, :, 2 * KV_WIDTH:3 * KV_WIDTH] = (
                pltpu.roll(k, HEAD_DIM, 1).astype(BF16).reshape(shape))
            kv_ref[:, :, 3 * KV_WIDTH:4 * KV_WIDTH] = (
                pltpu.roll(v, HEAD_DIM, 1).astype(BF16).reshape(shape))


    def project_in(sub, k):
        ub = utb_ref[sub * n_slab + k]
        pair = jnp.concatenate([ub[:PROWS], ub[PROWS:]], axis=1)
        hs = hs_refs[sub * SCAN_CHUNKS + k]
        for part in range(2):
            for j in range(tiles_per_half):
                n = part * (STATE_COLS // MXU_DIM) + k * tiles_per_half + j
                c0 = part * SCAN_COLS + j * MXU_DIM
                hs[SUBLANES:, c0:c0 + MXU_DIM] = _dot_f32_lhs(pair, wb_ref[0, n])

    def scan(sub, k):
        hs = hs_refs[sub * SCAN_CHUNKS + k]
        cols = slice(k * SCAN_COLS, (k + 1) * SCAN_COLS)
        ar = jnp.broadcast_to(are_ref[0, :, cols], (BATCH, SCAN_COLS))
        ai = jnp.broadcast_to(aim_ref[0, :, cols], (BATCH, SCAN_COLS))
        hr = state_ref[k, :, :SCAN_COLS]
        hi = state_ref[k, :, SCAN_COLS:]
        hs[:SUBLANES, :SCAN_COLS] = hr
        hs[:SUBLANES, SCAN_COLS:] = hi
        for t in range(PAIRS):
            rows = slice(SUBLANES + t * BATCH, SUBLANES + (t + 1) * BATCH)
            nr = ar * hr - ai * hi + hs[rows, :SCAN_COLS]
            ni = ar * hi + ai * hr + hs[rows, SCAN_COLS:]
            hs[rows, :SCAN_COLS] = nr
            hs[rows, SCAN_COLS:] = ni
            hr, hi = nr, ni
        state_ref[k, :, :SCAN_COLS] = hr
        state_ref[k, :, SCAN_COLS:] = hi

    def project_out(sub, n):
        chunks = range(n * per, (n + 1) * per)
        bufs = [hs_refs[sub * SCAN_CHUNKS + k] for k in chunks]
        h_prev = jnp.concatenate([hs[:PROWS, :] for hs in bufs], axis=1)
        h_odd = jnp.concatenate([hs[SUBLANES:, :] for hs in bufs], axis=1)
        u_even = jnp.concatenate([utb_ref[sub * n_slab + k][:PROWS] for k in chunks], axis=1)
        w_odd = wc_ref[0, n * per:(n + 1) * per].reshape(per * 2 * SCAN_COLS, MXU_DIM)
        w_even = wca_ref[0, n * per:(n + 1) * per].reshape(per * 2 * SCAN_COLS, MXU_DIM)
        y_even = _dot_f32_lhs(h_prev, w_even) + _dot_f32_lhs(u_even, wf_ref[0, n])
        return jnp.concatenate([y_even, _dot_f32_lhs(h_odd, w_odd)], axis=0)

    def finish(sub, ys):
        u = jnp.concatenate([utb_ref[sub * n_slab + s] for s in range(n_slab)], axis=1)
        y = jnp.concatenate(ys, axis=1) + d_ref[0] * u
        z = jax.nn.gelu(y)
        gate = jax.nn.sigmoid(
            jnp.dot(z.astype(BF16), wglub_ref[...], preferred_element_type=F32) + bglu_ref[0])
        res = _rms(z * gate, g_ref[0])
        for s in range(n_slab):
            otb_ref[sub * n_slab + s] = res[:, s * LANES:(s + 1) * LANES]
        t0 = sub * TT_SSM
        for b in range(BATCH):
            for s in range(n_slab):
                for par in range(2):
                    ost_ref[b * n_slab + s, pl.ds(t0 + par, PAIRS, stride=2), :] = (
                        otb_ref[sub * n_slab + s, pl.ds(par * PROWS + b, PAIRS, stride=BATCH), :])
                c0 = b * SSM_WIDTH + s * LANES
                o_ref[t0:t0 + TT_SSM, c0:c0 + LANES] = (
                    ost_ref[b * n_slab + s, t0:t0 + TT_SSM, :].astype(BF16))

    pending = None
    qkv = [0, 1]
    for sub in range(SSM_SUB):
        project_in(sub, 0)
        for k in range(SCAN_CHUNKS):
            if k + 1 < SCAN_CHUNKS:
                project_in(sub, k + 1)
            if pending is not None and k < len(pending[1]):
                pending[2].append(project_out(pending[0], pending[1][k]))
            elif qkv and k % 2 == 1:
                qkv_piece(qkv.pop(0))
            scan(sub, k)
        if pending is not None:
            finish(pending[0], pending[2])
        pending = (sub, list(range(SSM_WIDTH // MXU_DIM)), [])
    finish(pending[0], [project_out(pending[0], n) for n in pending[1]])
    for c in qkv:
        qkv_piece(c)


def _mix(layer, x, mod, pre_g, w_in, wb, a_re, a_im, wc, wca, wf, d_skip, w_glu, b_glu, ssm_g):
    rows = TT_SSM * BATCH
    vec = lambda n: pl.BlockSpec((1, 1, n), lambda i: (layer, 0, 0))
    return pl.pallas_call(
        _mix_kernel,
        grid=(SEQ // TT_MIX,),
        in_specs=[
            pl.BlockSpec((BATCH, TT_MIX, D_MODEL), lambda i: (0, i, 0)),
            pl.BlockSpec((1, BATCH, N_MOD, D_MODEL), lambda i: (layer, 0, 0, 0)),
            vec(D_MODEL),
            pl.BlockSpec((1, D_MODEL, IN_WIDTH), lambda i: (layer, 0, 0)),
            pl.BlockSpec((1,) + wb.shape[1:], lambda i: (layer, 0, 0, 0)),
            vec(STATE_COLS), vec(STATE_COLS),
            pl.BlockSpec((1,) + wc.shape[1:], lambda i: (layer, 0, 0, 0)),
            pl.BlockSpec((1,) + wca.shape[1:], lambda i: (layer, 0, 0, 0)),
            pl.BlockSpec((1,) + wf.shape[1:], lambda i: (layer, 0, 0, 0)),
            vec(SSM_WIDTH),
            pl.BlockSpec((1, SSM_WIDTH, SSM_WIDTH), lambda i: (layer, 0, 0)),
            vec(SSM_WIDTH), vec(SSM_WIDTH),
        ],
        out_specs=[
            pl.BlockSpec((BATCH, TT_MIX, ATTN_WIDTH), lambda i: (0, i, 0)),
            pl.BlockSpec((BATCH, TT_MIX, 4 * KV_WIDTH), lambda i: (0, i, 0)),
            pl.BlockSpec((TT_MIX, BATCH * SSM_WIDTH), lambda i: (i, 0)),
        ],
        out_shape=[
            jax.ShapeDtypeStruct((BATCH, SEQ, ATTN_WIDTH), BF16),
            jax.ShapeDtypeStruct((BATCH, SEQ, 4 * KV_WIDTH), BF16),
            jax.ShapeDtypeStruct((SEQ, BATCH * SSM_WIDTH), BF16),
        ],
        scratch_shapes=[
            pltpu.VMEM((BATCH * TT_MIX, D_MODEL), BF16),
            pltpu.VMEM((D_MODEL, IN_WIDTH), BF16),
            pltpu.VMEM((SSM_WIDTH // LANES, BATCH * TT_MIX, LANES), F32),
            pltpu.VMEM((SSM_SUB * SSM_WIDTH // LANES, rows, LANES), F32),
            pltpu.VMEM((SSM_SUB * SSM_WIDTH // LANES, rows, LANES), F32),
            pltpu.VMEM((BATCH * SSM_WIDTH // LANES, TT_MIX, LANES), F32),
            pltpu.VMEM((SCAN_CHUNKS, BATCH, 2 * SCAN_COLS), F32),
            pltpu.VMEM((SSM_WIDTH, SSM_WIDTH), BF16),
        ] + [pltpu.VMEM((SUBLANES + PROWS, 2 * SCAN_COLS), F32)
             for _ in range(SSM_SUB * SCAN_CHUNKS)],
        compiler_params=pltpu.CompilerParams(
            dimension_semantics=("arbitrary",), vmem_limit_bytes=VMEM_LIMIT),
        name="mix_in_s5",
    )(x, mod, pre_g, w_in, wb, a_re, a_im, wc, wca, wf, d_skip, w_glu, b_glu, ssm_g)


def _attn_kernel(q_ref, kvc_ref, kvp_ref, ssm_ref, x_ref, mod_ref, sink_ref, ag_ref, wout_ref,
                 pg_ref, w1_ref, w2_ref, o_ref, w1b_ref, w2b_ref, kpad_ref, vpad_ref, bias_ref,
                 heads_ref, st_ref, p_ref, attn_ref, mixed_ref, woutb_ref):
    i = pl.program_id(1)
    nq = TQ_ATT // WINDOW

    w1b_ref[0] = w1_ref[0].astype(BF16)
    w2b_ref[0] = w2_ref[0].astype(BF16)

    @pl.when((pl.program_id(0) == 0) & (i == 0))
    def _():
        j = lax.broadcasted_iota(jnp.int32, (2 * WINDOW, WINDOW), 0)
        r = lax.broadcasted_iota(jnp.int32, (2 * WINDOW, WINDOW), 1)
        diff = WINDOW + r - j
        valid = (diff >= 0) & (diff < WINDOW)
        for h in range(N_Q_HEADS):
            slope = 2.0 ** (-8.0 * (h + 1) / N_Q_HEADS)
            bias = -(slope * LOG2E) * diff.astype(F32)
            bias_ref[h] = jnp.where(valid, bias, NEG_INF)
            bias_ref[N_Q_HEADS + h] = jnp.where(valid & (j >= WINDOW), bias, NEG_INF)
        woutb_ref[...] = wout_ref[0].astype(BF16)

    lane = lax.broadcasted_iota(jnp.int32, (1, LANES), 1)
    for src, rows in ((kvp_ref, slice(0, WINDOW)), (kvc_ref, slice(WINDOW, WINDOW + TQ_ATT))):
        for hk in range(N_KV_HEADS):
            for par in range(2):
                keep = (lane >= par * HEAD_DIM) & (lane < (par + 1) * HEAD_DIM)
                off = 0 if par == hk else 2 * KV_WIDTH
                kpad_ref[hk * 2 + par, rows, :] = jnp.where(
                    keep, src[0, :, off:off + KV_WIDTH], jnp.zeros((), BF16))
                vpad_ref[hk * 2 + par, rows, :] = jnp.where(
                    keep, src[0, :, off + KV_WIDTH:off + 2 * KV_WIDTH], jnp.zeros((), BF16))

    nt = (((1,), (1,)), ((), ()))
    tn = (((0,), (0,)), ((), ()))

    def scores(jb):
        q0 = jb * WINDOW
        for hk in range(N_KV_HEADS):
            qq = jnp.concatenate(
                [q_ref[0, q0:q0 + WINDOW, (2 * hk + l2) * LANES:(2 * hk + l2 + 1) * LANES]
                 for l2 in range(2)], axis=0)
            for par in range(2):
                kp = kpad_ref[hk * 2 + par, q0:q0 + 2 * WINDOW, :]
                st_ref[(jb % 2) * 2 * N_KV_HEADS + hk * 2 + par] = lax.dot_general(
                    kp, qq, nt, preferred_element_type=F32)

    def softmax(jb):
        table = jnp.where(i == 0, N_Q_HEADS, 0) if jb == 0 else 0
        for hk in range(N_KV_HEADS):
            for par in range(2):
                for l2 in range(2):
                    h = Q_PER_KV * hk + 2 * l2 + par
                    buf = (jb % 2) * 2 * N_KV_HEADS + hk * 2 + par
                    cols = slice(l2 * WINDOW, (l2 + 1) * WINDOW)
                    sink = sink_ref[0, 0, h] * LOG2E
                    m = None
                    for c in range(0, 2 * WINDOW, SOFTMAX_ROWS):
                        rows = slice(c, c + SOFTMAX_ROWS)
                        sc = st_ref[buf, rows, cols] + bias_ref[table + h, rows, :]
                        st_ref[buf, rows, cols] = sc
                        cm = jnp.max(sc.reshape(SOFTMAX_ROWS // SUBLANES, SUBLANES, WINDOW), axis=0)
                        m = cm if m is None else jnp.maximum(m, cm)
                    m = jnp.maximum(jnp.max(m, axis=0, keepdims=True), sink)
                    part = None
                    for c in range(0, 2 * WINDOW, SOFTMAX_ROWS):
                        rows = slice(c, c + SOFTMAX_ROWS)
                        e = jnp.exp2(st_ref[buf, rows, cols] - m)
                        st_ref[buf, rows, cols] = e
                        ce = jnp.sum(e.reshape(SOFTMAX_ROWS // SUBLANES, SUBLANES, WINDOW), axis=0)
                        part = ce if part is None else part + ce
                    inv = 1.0 / (jnp.sum(part, axis=0, keepdims=True) + jnp.exp2(sink - m))
                    for c in range(0, 2 * WINDOW, SOFTMAX_ROWS):
                        rows = slice(c, c + SOFTMAX_ROWS)
                        p_ref[(jb % 2) * N_Q_HEADS + h, rows, :] = (
                            st_ref[buf, rows, cols] * inv).astype(BF16)

    def values(jb):
        q0 = jb * WINDOW
        for lt in range(N_Q_HEADS // 2):
            hk = 2 * lt // Q_PER_KV
            acc = None
            for par in range(2):
                vp = vpad_ref[hk * 2 + par, q0:q0 + 2 * WINDOW, :]
                o = lax.dot_general(p_ref[(jb % 2) * N_Q_HEADS + 2 * lt + par], vp, tn,
                                    preferred_element_type=F32)
                acc = o if acc is None else acc + o
            heads_ref[q0:q0 + WINDOW, lt * LANES:(lt + 1) * LANES] = acc

    mod = mod_ref[0, 0]
    gate = mod[2:3] * pg_ref[0]

    def out_proj_pieces(part):
        rows = slice(part * (TQ_ATT // ATT_SUB), (part + 1) * (TQ_ATT // ATT_SUB))
        n_col = D_MODEL // MXU_DIM

        def piece(c):
            if c == 0:
                attn_ref[rows, :] = _rms(heads_ref[rows, :], ag_ref[0]).astype(BF16)
            cols = slice(c * MXU_DIM, (c + 1) * MXU_DIM)
            heads = jnp.concatenate([attn_ref[rows, :], ssm_ref[rows, :]], axis=1)
            mixed_ref[rows, cols] = jnp.dot(heads, woutb_ref[:, cols], preferred_element_type=F32)
            if c == n_col - 1:
                o_ref[0, rows, :] = x_ref[0, rows, :] + _rms(mixed_ref[rows, :], gate)

        return [functools.partial(piece, c) for c in range(n_col)]

    per_part = nq // ATT_SUB
    queue = []
    scores(0)
    for jb in range(nq):
        if jb + 1 < nq:
            scores(jb + 1)
        softmax(jb)
        values(jb)
        for _ in range(-(-len(queue) // (per_part - jb % per_part))):
            queue.pop(0)()
        if (jb + 1) % per_part == 0:
            queue.extend(out_proj_pieces(jb // per_part))
    for piece in queue:
        piece()


def _attn(layer, q, kv, ssm_tm, x, mod, sinks, attn_g, w_out, post_g, w1, w2):
    nt = SEQ // TQ_ATT
    per_tile = TQ_ATT // WINDOW
    steps = BATCH * nt
    vec = lambda n: pl.BlockSpec((1, 1, n), lambda b, i: (layer, 0, 0))
    slab = lambda rows, cols: pl.BlockSpec(
        (1, rows // steps, cols), lambda b, i: (layer, b * nt + i, 0))
    slab_out = lambda rows, cols: pl.BlockSpec(
        (1, rows // steps, cols), lambda b, i: (0, b * nt + i, 0))
    return pl.pallas_call(
        _attn_kernel,
        grid=(BATCH, nt),
        in_specs=[
            pl.BlockSpec((1, TQ_ATT, ATTN_WIDTH), lambda b, i: (b, i, 0)),
            pl.BlockSpec((1, TQ_ATT, 4 * KV_WIDTH), lambda b, i: (b, i, 0)),
            pl.BlockSpec((1, WINDOW, 4 * KV_WIDTH),
                         lambda b, i: (b, jnp.maximum(i * per_tile - 1, 0), 0)),
            pl.BlockSpec((TQ_ATT, SSM_WIDTH), lambda b, i: (i, b)),
            pl.BlockSpec((1, TQ_ATT, D_MODEL), lambda b, i: (b, i, 0)),
            pl.BlockSpec((1, 1, N_MOD, D_MODEL), lambda b, i: (layer, b, 0, 0)),
            pl.BlockSpec((1, 1, N_Q_HEADS), lambda b, i: (layer, 0, 0), memory_space=pltpu.SMEM),
            vec(ATTN_WIDTH),
            pl.BlockSpec((1, D_MODEL, D_MODEL), lambda b, i: (layer, 0, 0)),
            vec(D_MODEL),
            slab(D_MODEL, D_FF),
            slab(D_FF, D_MODEL),
        ],
        out_specs=[
            pl.BlockSpec((1, TQ_ATT, D_MODEL), lambda b, i: (b, i, 0)),
            slab_out(D_MODEL, D_FF),
            slab_out(D_FF, D_MODEL),
        ],
        out_shape=[
            jax.ShapeDtypeStruct((BATCH, SEQ, D_MODEL), F32),
            jax.ShapeDtypeStruct((1, D_MODEL, D_FF), BF16),
            jax.ShapeDtypeStruct((1, D_FF, D_MODEL), BF16),
        ],
        scratch_shapes=[
            pltpu.VMEM((2 * N_KV_HEADS, WINDOW + TQ_ATT, LANES), BF16),
            pltpu.VMEM((2 * N_KV_HEADS, WINDOW + TQ_ATT, LANES), BF16),
            pltpu.VMEM((2 * N_Q_HEADS, 2 * WINDOW, WINDOW), F32),
            pltpu.VMEM((TQ_ATT, ATTN_WIDTH), F32),
            pltpu.VMEM((2 * 2 * N_KV_HEADS, 2 * WINDOW, 2 * WINDOW), F32),
            pltpu.VMEM((2 * N_Q_HEADS, 2 * WINDOW, WINDOW), BF16),
            pltpu.VMEM((TQ_ATT, ATTN_WIDTH), BF16),
            pltpu.VMEM((TQ_ATT, D_MODEL), F32),
            pltpu.VMEM((D_MODEL, D_MODEL), BF16),
        ],
        compiler_params=pltpu.CompilerParams(
            dimension_semantics=("arbitrary", "arbitrary"), vmem_limit_bytes=VMEM_LIMIT),
        name="attn_outproj",
    )(q, kv, kv, ssm_tm, x, mod, sinks, attn_g, w_out, post_g, w1, w2)


def _mlp_kernel(x_ref, mod_ref, g_ref, w1_ref, w2_ref, pg_ref, o_ref, acc_ref):
    x = x_ref[0]
    mod = mod_ref[0, 0]
    h = (_rms(x, g_ref[0] * (1.0 + mod[4:5])) + mod[3:4]).astype(BF16)
    for c in range(D_FF // FF_CHUNK):
        cols = slice(c * FF_CHUNK, (c + 1) * FF_CHUNK)
        f = jnp.dot(h, w1_ref[0, :, cols], preferred_element_type=F32)
        f = jnp.square(jnp.maximum(f, 0.0)).astype(BF16)
        part = jnp.dot(f, w2_ref[0, cols, :], preferred_element_type=F32)
        if c == 0:
            acc_ref[...] = part
        else:
            acc_ref[...] += part
    o_ref[0] = x + _rms(acc_ref[...], mod[5:6] * pg_ref[0])


def _mlp(layer, x, mod, pre_g, w1, w2, post_g):
    nt = SEQ // TM_MLP
    vec = lambda n: pl.BlockSpec((1, 1, n), lambda b, i: (layer, 0, 0))
    return pl.pallas_call(
        _mlp_kernel,
        grid=(BATCH, nt),
        in_specs=[
            pl.BlockSpec((1, TM_MLP, D_MODEL), lambda b, i: (b, i, 0)),
            pl.BlockSpec((1, 1, N_MOD, D_MODEL), lambda b, i: (layer, b, 0, 0)),
            vec(D_MODEL),
            pl.BlockSpec((1, D_MODEL, D_FF), lambda b, i: (0, 0, 0), pipeline_mode=pl.Buffered(1)),
            pl.BlockSpec((1, D_FF, D_MODEL), lambda b, i: (0, 0, 0), pipeline_mode=pl.Buffered(1)),
            vec(D_MODEL),
        ],
        out_specs=pl.BlockSpec((1, TM_MLP, D_MODEL), lambda b, i: (b, i, 0)),
        out_shape=jax.ShapeDtypeStruct((BATCH, SEQ, D_MODEL), F32),
        scratch_shapes=[pltpu.VMEM((TM_MLP, D_MODEL), F32)],
        compiler_params=pltpu.CompilerParams(
            dimension_semantics=("arbitrary", "arbitrary"), vmem_limit_bytes=VMEM_LIMIT),
        name="mlp",
    )(x, mod, pre_g, w1, w2, post_g)


def _ssm_params(lam_re, lam_im, log_dt, b_re, b_im, c_re, c_im):
    dt = jnp.exp(log_dt)[..., None]
    mag = jnp.exp(lam_re * dt)
    ang = lam_im * dt
    ab_r = mag * jnp.cos(ang)
    ab_i = mag * jnp.sin(ang)
    nr = ab_r - 1.0
    ni = ab_i
    den = lam_re * lam_re + lam_im * lam_im
    f_r = (nr * lam_re + ni * lam_im) / den
    f_i = (ni * lam_re - nr * lam_im) / den
    bb_r = f_r[..., None] * b_re - f_i[..., None] * b_im
    bb_i = f_r[..., None] * b_im + f_i[..., None] * b_re
    gpt = MXU_DIM // STATE
    gps = LANES // SSM_GROUP
    n_t = STATE_COLS // MXU_DIM
    shape_b = (n_t, LANES, MXU_DIM)
    n_i = lax.broadcasted_iota(jnp.int32, shape_b, 0)
    g_i = lax.broadcasted_iota(jnp.int32, shape_b, 1) // SSM_GROUP
    h_i = lax.broadcasted_iota(jnp.int32, shape_b, 2) // STATE
    mask_b = g_i == gpt * (n_i % (gps // gpt)) + h_i

    def b_tiles(bb):
        blk = bb.reshape(DEPTH, n_t, gpt, STATE, SSM_GROUP).transpose(0, 1, 4, 2, 3)
        blk = blk.reshape(DEPTH, n_t, 1, SSM_GROUP, MXU_DIM)
        rep = jnp.broadcast_to(blk, (DEPTH, n_t, gps, SSM_GROUP, MXU_DIM))
        return jnp.where(mask_b, rep.reshape(DEPTH, n_t, LANES, MXU_DIM), 0.0)

    ba_r = ab_r[..., None] * bb_r - ab_i[..., None] * bb_i
    ba_i = ab_r[..., None] * bb_i + ab_i[..., None] * bb_r
    pair_tiles = lambda first, second: jnp.concatenate([b_tiles(first), b_tiles(second)], axis=2)
    wb = jnp.concatenate([pair_tiles(ba_r, bb_r), pair_tiles(ba_i, bb_i)],
                         axis=1).astype(BF16)

    gpc = MXU_DIM // SSM_GROUP
    shape_c = (SCAN_CHUNKS, SCAN_COLS, MXU_DIM)
    k_i = lax.broadcasted_iota(jnp.int32, shape_c, 0)
    r_i = lax.broadcasted_iota(jnp.int32, shape_c, 1) // STATE
    c_i = lax.broadcasted_iota(jnp.int32, shape_c, 2) // SSM_GROUP
    mask_c = c_i == (gps * k_i + r_i) % gpc

    def c_rows(cc):
        blk = cc.transpose(0, 1, 3, 2).reshape(DEPTH, SCAN_CHUNKS, SCAN_COLS, 1, SSM_GROUP)
        rep = jnp.broadcast_to(blk, (DEPTH, SCAN_CHUNKS, SCAN_COLS, gpc, SSM_GROUP))
        return jnp.where(mask_c, rep.reshape(DEPTH, SCAN_CHUNKS, SCAN_COLS, MXU_DIM), 0.0)

    wc = jnp.concatenate([c_rows(c_re), c_rows(-c_im)], axis=2).astype(BF16)
    ca_r = c_re * ab_r[:, :, None, :] - c_im * ab_i[:, :, None, :]
    ca_i = c_re * ab_i[:, :, None, :] + c_im * ab_r[:, :, None, :]
    wca = jnp.concatenate([c_rows(ca_r), c_rows(-ca_i)], axis=2).astype(BF16)
    hp = lax.Precision.HIGHEST
    feed = (jnp.einsum('lgcp,lgpd->lgdc', c_re, bb_r, precision=hp)
            - jnp.einsum('lgcp,lgpd->lgdc', c_im, bb_i, precision=hp))
    n_out = SSM_WIDTH // MXU_DIM
    blk = feed.reshape(DEPTH, n_out, gpc, SSM_GROUP, 1, SSM_GROUP)
    rep = jnp.broadcast_to(blk, (DEPTH, n_out, gpc, SSM_GROUP, gpc, SSM_GROUP))
    r_g = lax.broadcasted_iota(jnp.int32, (MXU_DIM, MXU_DIM), 0) // SSM_GROUP
    c_g = lax.broadcasted_iota(jnp.int32, (MXU_DIM, MXU_DIM), 1) // SSM_GROUP
    wf = jnp.where(r_g == c_g, rep.reshape(DEPTH, n_out, MXU_DIM, MXU_DIM), 0.0).astype(BF16)
    a2_r = (ab_r * ab_r - ab_i * ab_i).reshape(DEPTH, 1, STATE_COLS)
    a2_i = (2.0 * ab_r * ab_i).reshape(DEPTH, 1, STATE_COLS)
    return wb, a2_r, a2_i, wc, wca, wf


def kernel(x, c, w_ada, b_ada, pre_mix_g, w_in, attn_sinks, lam_re, lam_im, log_dt, b_re, b_im,
           c_re, c_im, d_skip, w_glu, b_glu, attn_out_g, ssm_out_g, w_out, post_mix_g, pre_mlp_g,
           w_mlp_in, w_mlp_out, post_mlp_g):
    row = lambda a: a.reshape(DEPTH, 1, a.shape[-1])
    mod = _modulation(c, w_ada, b_ada).reshape(DEPTH, BATCH, N_MOD, D_MODEL)
    wb, a_re, a_im, wc, wca, wf = _ssm_params(lam_re, lam_im, log_dt, b_re, b_im, c_re, c_im)
    sinks = row(attn_sinks)
    for layer in range(DEPTH):
        q, kv, ssm_tm = _mix(layer, x, mod, row(pre_mix_g), w_in, wb, a_re, a_im, wc, wca, wf,
                             row(d_skip), w_glu, row(b_glu), row(ssm_out_g))
        x, w1_b, w2_b = _attn(layer, q, kv, ssm_tm, x, mod, sinks, row(attn_out_g), w_out,
                              row(post_mix_g), w_mlp_in, w_mlp_out)
        x = _mlp(layer, x, mod, row(pre_mlp_g), w1_b, w2_b, row(post_mlp_g))
    return x
```

```python
import functools

import jax
import jax.numpy as jnp
from jax import lax
from jax.experimental import pallas as pl
from jax.experimental.pallas import tpu as pltpu

D_MODEL = 1024
BATCH = 8
SEQ = 4096
DEPTH = 4
ATTN_WIDTH = 512
SSM_WIDTH = 512
HEAD_DIM = 64
N_Q_HEADS = 8
N_KV_HEADS = 2
Q_PER_KV = 4
KV_WIDTH = 128
WINDOW = 128
SSM_GROUP = 16
N_SSM_GROUPS = 32
STATE = 64
D_FF = 4096
IN_WIDTH = 1280
N_MOD = 6
EPS = 1e-6
NEG_INF = -1e30
LOG2E = 1.4426950408889634

LANES = 128
SUBLANES = 8
MXU_DIM = 256

STATE_COLS = N_SSM_GROUPS * STATE
BF16 = jnp.bfloat16
F32 = jnp.float32

VMEM_LIMIT = 56 * 1024 * 1024

TT_SSM = 64
SSM_SUB = 2
TQ_ATT = 1024
ATT_SUB = 2
TM_MLP = 1024
FF_CHUNK = 1024
SCAN_COLS = LANES * STATE // SSM_GROUP
SCAN_CHUNKS = STATE_COLS // SCAN_COLS


def _rms(x, g):
    return x * lax.rsqrt(jnp.mean(x * x, axis=-1, keepdims=True) + EPS) * g


def _dot_f32_lhs(a, w):
    return lax.dot_general(a, w, (((1,), (0,)), ((), ())), preferred_element_type=F32)


def _mod_kernel(c_ref, w_ref, b_ref, o_ref):
    c = c_ref[...]
    ca = (c * jax.nn.sigmoid(c)).astype(BF16)
    o_ref[0] = jnp.dot(ca, w_ref[0].astype(BF16), preferred_element_type=F32) + b_ref[0]


def _modulation(c, w_ada, b_ada):
    nb = 1536
    return pl.pallas_call(
        _mod_kernel,
        grid=(DEPTH, N_MOD * D_MODEL // nb),
        in_specs=[
            pl.BlockSpec((BATCH, D_MODEL), lambda l, j: (0, 0)),
            pl.BlockSpec((1, D_MODEL, nb), lambda l, j: (l, 0, j)),
            pl.BlockSpec((1, 1, nb), lambda l, j: (l, 0, j)),
        ],
        out_specs=pl.BlockSpec((1, BATCH, nb), lambda l, j: (l, 0, j)),
        out_shape=jax.ShapeDtypeStruct((DEPTH, BATCH, N_MOD * D_MODEL), F32),
        compiler_params=pltpu.CompilerParams(
            dimension_semantics=("arbitrary", "arbitrary"), vmem_limit_bytes=VMEM_LIMIT),
        name="adaln_mod",
    )(c, w_ada, b_ada.reshape(DEPTH, 1, N_MOD * D_MODEL))


TT_MIX = SSM_SUB * TT_SSM
PAIRS = TT_SSM // 2
PROWS = PAIRS * BATCH


def _mix_kernel(x_ref, mod_ref, pg_ref, win_ref, wb_ref, are_ref, aim_ref, wc_ref, wca_ref, wf_ref,
                d_ref, wglu_ref, bglu_ref, g_ref, q_ref, kv_ref, o_ref,
                hn_ref, winb_ref, uall_ref, utb_ref, otb_ref, ost_ref, state_ref, wglub_ref,
                *hs_refs):
    @pl.when(pl.program_id(0) == 0)
    def _():
        state_ref[...] = jnp.zeros_like(state_ref)
        wglub_ref[...] = wglu_ref[0].astype(BF16)
        winb_ref[...] = win_ref[0].astype(BF16)

    n_slab = SSM_WIDTH // LANES
    tiles_per_half = SCAN_COLS // MXU_DIM
    per = SCAN_CHUNKS // (SSM_WIDTH // MXU_DIM)
    u_col = ATTN_WIDTH + 2 * KV_WIDTH

    half = BATCH // 2
    for part in range(2):
        for b in range(part * half, (part + 1) * half):
            mod = mod_ref[0, b]
            hn_ref[b * TT_MIX:(b + 1) * TT_MIX, :] = (
                _rms(x_ref[b], pg_ref[0] * (1.0 + mod[1:2])) + mod[0:1]).astype(BF16)
        rows = slice(part * half * TT_MIX, (part + 1) * half * TT_MIX)
        u = jnp.dot(hn_ref[rows, :], winb_ref[:, u_col:], preferred_element_type=F32)
        for s in range(n_slab):
            uall_ref[s, rows, :] = u[:, s * LANES:(s + 1) * LANES]

    for sub in range(SSM_SUB):
        for b in range(BATCH):
            for par in range(2):
                r0 = b * TT_MIX + sub * TT_SSM + par
                for s in range(n_slab):
                    utb_ref[sub * n_slab + s, pl.ds(par * PROWS + b, PAIRS, stride=BATCH), :] = (
                        uall_ref[s, pl.ds(r0, PAIRS, stride=2), :])

    def qkv_piece(c):
        if c == 0:
            q = jnp.dot(hn_ref[...], winb_ref[:, :ATTN_WIDTH], preferred_element_type=F32)
            q_ref[...] = (q * (HEAD_DIM ** -0.5 * LOG2E)).astype(BF16).reshape(
                BATCH, TT_MIX, ATTN_WIDTH)
        else:
            kv = jnp.dot(hn_ref[...], winb_ref[:, ATTN_WIDTH:u_col], preferred_element_type=F32)
            k = kv[:, :KV_WIDTH]
            v = kv[:, KV_WIDTH:]
            shape = (BATCH, TT_MIX, KV_WIDTH)
            kv_ref[:, :, 0 * KV_WIDTH:1 * KV_WIDTH] = k.astype(BF16).reshape(shape)
            kv_ref[:, :, 1 * KV_WIDTH:2 * KV_WIDTH] = v.astype(BF16).reshape(shape)
            kv_ref[:, :, 2 * KV_WIDTH:3 * KV_WIDTH] = (
                pltpu.roll(k, HEAD_DIM, 1).astype(BF16).reshape(shape))
            kv_ref[:, :, 3 * KV_WIDTH:4 * KV_WIDTH] = (
                pltpu.roll(v, HEAD_DIM, 1).astype(BF16).reshape(shape))


    def project_in(sub, k):
        ub = utb_ref[sub * n_slab + k]
        pair = jnp.concatenate([ub[:PROWS], ub[PROWS:]], axis=1)
        hs = hs_refs[sub * SCAN_CHUNKS + k]
        for part in range(2):
            for j in range(tiles_per_half):
                n = part * (STATE_COLS // MXU_DIM) + k * tiles_per_half + j
                c0 = part * SCAN_COLS + j * MXU_DIM
                hs[SUBLANES:, c0:c0 + MXU_DIM] = _dot_f32_lhs(pair, wb_ref[0, n])

    def scan(sub, k):
        hs = hs_refs[sub * SCAN_CHUNKS + k]
        cols = slice(k * SCAN_COLS, (k + 1) * SCAN_COLS)
        ar = jnp.broadcast_to(are_ref[0, :, cols], (BATCH, SCAN_COLS))
        ai = jnp.broadcast_to(aim_ref[0, :, cols], (BATCH, SCAN_COLS))
        hr = state_ref[k, :, :SCAN_COLS]
        hi = state_ref[k, :, SCAN_COLS:]
        hs[:SUBLANES, :SCAN_COLS] = hr
        hs[:SUBLANES, SCAN_COLS:] = hi
        for t in range(PAIRS):
            rows = slice(SUBLANES + t * BATCH, SUBLANES + (t + 1) * BATCH)
            nr = ar * hr - ai * hi + hs[rows, :SCAN_COLS]
            ni = ar * hi + ai * hr + hs[rows, SCAN_COLS:]
            hs[rows, :SCAN_COLS] = nr
            hs[rows, SCAN_COLS:] = ni
            hr, hi = nr, ni
        state_ref[k, :, :SCAN_COLS] = hr
        state_ref[k, :, SCAN_COLS:] = hi

    def project_out(sub, n):
        chunks = range(n * per, (n + 1) * per)
        bufs = [hs_refs[sub * SCAN_CHUNKS + k] for k in chunks]
        h_prev = jnp.concatenate([hs[:PROWS, :] for hs in bufs], axis=1)
        h_odd = jnp.concatenate([hs[SUBLANES:, :] for hs in bufs], axis=1)
        u_even = jnp.concatenate([utb_ref[sub * n_slab + k][:PROWS] for k in chunks], axis=1)
        w_odd = wc_ref[0, n * per:(n + 1) * per].reshape(per * 2 * SCAN_COLS, MXU_DIM)
        w_even = wca_ref[0, n * per:(n + 1) * per].reshape(per * 2 * SCAN_COLS, MXU_DIM)
        y_even = _dot_f32_lhs(h_prev, w_even) + _dot_f32_lhs(u_even, wf_ref[0, n])
        return jnp.concatenate([y_even, _dot_f32_lhs(h_odd, w_odd)], axis=0)

    def finish(sub, ys):
        u = jnp.concatenate([utb_ref[sub * n_slab + s] for s in range(n_slab)], axis=1)
        y = jnp.concatenate(ys, axis=1) + d_ref[0] * u
        z = jax.nn.gelu(y)
        gate = jax.nn.sigmoid(
            jnp.dot(z.astype(BF16), wglub_ref[...], preferred_element_type=F32) + bglu_ref[0])
        res = _rms(z * gate, g_ref[0])
        for s in range(n_slab):
            otb_ref[sub * n_slab + s] = res[:, s * LANES:(s + 1) * LANES]
        t0 = sub * TT_SSM
        for b in range(BATCH):
            for s in range(n_slab):
                for par in range(2):
                    ost_ref[b * n_slab + s, pl.ds(t0 + par, PAIRS, stride=2), :] = (
                        otb_ref[sub * n_slab + s, pl.ds(par * PROWS + b, PAIRS, stride=BATCH), :])
                c0 = b * SSM_WIDTH + s * LANES
                o_ref[t0:t0 + TT_SSM, c0:c0 + LANES] = (
                    ost_ref[b * n_slab + s, t0:t0 + TT_SSM, :].astype(BF16))

    pending = None
    qkv = [0, 1]
    for sub in range(SSM_SUB):
        project_in(sub, 0)
        for k in range(SCAN_CHUNKS):
            if k + 1 < SCAN_CHUNKS:
                project_in(sub, k + 1)
            if pending is not None and k < len(pending[1]):
                pending[2].append(project_out(pending[0], pending[1][k]))
            elif qkv and k % 2 == 1:
                qkv_piece(qkv.pop(0))
            scan(sub, k)
        if pending is not None:
            finish(pending[0], pending[2])
        pending = (sub, list(range(SSM_WIDTH // MXU_DIM)), [])
    finish(pending[0], [project_out(pending[0], n) for n in pending[1]])
    for c in qkv:
        qkv_piece(c)


def _mix(layer, x, mod, pre_g, w_in, wb, a_re, a_im, wc, wca, wf, d_skip, w_glu, b_glu, ssm_g):
    rows = TT_SSM * BATCH
    vec = lambda n: pl.BlockSpec((1, 1, n), lambda i: (layer, 0, 0))
    return pl.pallas_call(
        _mix_kernel,
        grid=(SEQ // TT_MIX,),
        in_specs=[
            pl.BlockSpec((BATCH, TT_MIX, D_MODEL), lambda i: (0, i, 0)),
            pl.BlockSpec((1, BATCH, N_MOD, D_MODEL), lambda i: (layer, 0, 0, 0)),
            vec(D_MODEL),
            pl.BlockSpec((1, D_MODEL, IN_WIDTH), lambda i: (layer, 0, 0)),
            pl.BlockSpec((1,) + wb.shape[1:], lambda i: (layer, 0, 0, 0)),
            vec(STATE_COLS), vec(STATE_COLS),
            pl.BlockSpec((1,) + wc.shape[1:], lambda i: (layer, 0, 0, 0)),
            pl.BlockSpec((1,) + wca.shape[1:], lambda i: (layer, 0, 0, 0)),
            pl.BlockSpec((1,) + wf.shape[1:], lambda i: (layer, 0, 0, 0)),
            vec(SSM_WIDTH),
            pl.BlockSpec((1, SSM_WIDTH, SSM_WIDTH), lambda i: (layer, 0, 0)),
            vec(SSM_WIDTH), vec(SSM_WIDTH),
        ],
        out_specs=[
            pl.BlockSpec((BATCH, TT_MIX, ATTN_WIDTH), lambda i: (0, i, 0)),
            pl.BlockSpec((BATCH, TT_MIX, 4 * KV_WIDTH), lambda i: (0, i, 0)),
            pl.BlockSpec((TT_MIX, BATCH * SSM_WIDTH), lambda i: (i, 0)),
        ],
        out_shape=[
            jax.ShapeDtypeStruct((BATCH, SEQ, ATTN_WIDTH), BF16),
            jax.ShapeDtypeStruct((BATCH, SEQ, 4 * KV_WIDTH), BF16),
            jax.ShapeDtypeStruct((SEQ, BATCH * SSM_WIDTH), BF16),
        ],
        scratch_shapes=[
            pltpu.VMEM((BATCH * TT_MIX, D_MODEL), BF16),
            pltpu.VMEM((D_MODEL, IN_WIDTH), BF16),
            pltpu.VMEM((SSM_WIDTH // LANES, BATCH * TT_MIX, LANES), F32),
            pltpu.VMEM((SSM_SUB * SSM_WIDTH // LANES, rows, LANES), F32),
            pltpu.VMEM((SSM_SUB * SSM_WIDTH // LANES, rows, LANES), F32),
            pltpu.VMEM((BATCH * SSM_WIDTH // LANES, TT_MIX, LANES), F32),
            pltpu.VMEM((SCAN_CHUNKS, BATCH, 2 * SCAN_COLS), F32),
            pltpu.VMEM((SSM_WIDTH, SSM_WIDTH), BF16),
        ] + [pltpu.VMEM((SUBLANES + PROWS, 2 * SCAN_COLS), F32)
             for _ in range(SSM_SUB * SCAN_CHUNKS)],
        compiler_params=pltpu.CompilerParams(
            dimension_semantics=("arbitrary",), vmem_limit_bytes=VMEM_LIMIT),
        name="mix_in_s5",
    )(x, mod, pre_g, w_in, wb, a_re, a_im, wc, wca, wf, d_skip, w_glu, b_glu, ssm_g)


def _attn_kernel(q_ref, kvc_ref, kvp_ref, ssm_ref, x_ref, mod_ref, sink_ref, ag_ref, wout_ref,
                 pg_ref, w1_ref, w2_ref, o_ref, w1b_ref, w2b_ref, kpad_ref, vpad_ref, bias_ref,
                 heads_ref, st_ref, p_ref, attn_ref, mixed_ref, woutb_ref):
    i = pl.program_id(1)
    nq = TQ_ATT // WINDOW

    w1b_ref[0] = w1_ref[0].astype(BF16)
    w2b_ref[0] = w2_ref[0].astype(BF16)

    @pl.when((pl.program_id(0) == 0) & (i == 0))
    def _():
        j = lax.broadcasted_iota(jnp.int32, (2 * WINDOW, WINDOW), 0)
        r = lax.broadcasted_iota(jnp.int32, (2 * WINDOW, WINDOW), 1)
        diff = WINDOW + r - j
        valid = (diff >= 0) & (diff < WINDOW)
        for h in range(N_Q_HEADS):
            slope = 2.0 ** (-8.0 * (h + 1) / N_Q_HEADS)
            bias = -(slope * LOG2E) * diff.astype(F32)
            bias_ref[h] = jnp.where(valid, bias, NEG_INF)
            bias_ref[N_Q_HEADS + h] = jnp.where(valid & (j >= WINDOW), bias, NEG_INF)
        woutb_ref[...] = wout_ref[0].astype(BF16)

    lane = lax.broadcasted_iota(jnp.int32, (1, LANES), 1)
    for src, rows in ((kvp_ref, slice(0, WINDOW)), (kvc_ref, slice(WINDOW, WINDOW + TQ_ATT))):
        for hk in range(N_KV_HEADS):
            for par in range(2):
                keep = (lane >= par * HEAD_DIM) & (lane < (par + 1) * HEAD_DIM)
                off = 0 if par == hk else 2 * KV_WIDTH
                kpad_ref[hk * 2 + par, rows, :] = jnp.where(
                    keep, src[0, :, off:off + KV_WIDTH], jnp.zeros((), BF16))
                vpad_ref[hk * 2 + par, rows, :] = jnp.where(
                    keep, src[0, :, off + KV_WIDTH:off + 2 * KV_WIDTH], jnp.zeros((), BF16))

    nt = (((1,), (1,)), ((), ()))
    tn = (((0,), (0,)), ((), ()))

    def scores(jb):
        q0 = jb * WINDOW
        for hk in range(N_KV_HEADS):
            qq = jnp.concatenate(
                [q_ref[0, q0:q0 + WINDOW, (2 * hk + l2) * LANES:(2 * hk + l2 + 1) * LANES]
                 for l2 in range(2)], axis=0)
            for par in range(2):
                kp = kpad_ref[hk * 2 + par, q0:q0 + 2 * WINDOW, :]
                st_ref[(jb % 2) * 2 * N_KV_HEADS + hk * 2 + par] = lax.dot_general(
                    kp, qq, nt, preferred_element_type=F32)

    def softmax(jb):
        table = jnp.where(i == 0, N_Q_HEADS, 0) if jb == 0 else 0
        for hk in range(N_KV_HEADS):
            for par in range(2):
                for l2 in range(2):
                    h = Q_PER_KV * hk + 2 * l2 + par
                    s = (st_ref[(jb % 2) * 2 * N_KV_HEADS + hk * 2 + par, :,
                                l2 * WINDOW:(l2 + 1) * WINDOW] + bias_ref[table + h])
                    sink = sink_ref[0, 0, h] * LOG2E
                    m = jnp.maximum(jnp.max(s, axis=0, keepdims=True), sink)
                    e = jnp.exp2(s - m)
                    denom = jnp.sum(e, axis=0, keepdims=True) + jnp.exp2(sink - m)
                    p_ref[(jb % 2) * N_Q_HEADS + h] = (e * (1.0 / denom)).astype(BF16)

    def values(jb):
        q0 = jb * WINDOW
        for lt in range(N_Q_HEADS // 2):
            hk = 2 * lt // Q_PER_KV
            acc = None
            for par in range(2):
                vp = vpad_ref[hk * 2 + par, q0:q0 + 2 * WINDOW, :]
                o = lax.dot_general(p_ref[(jb % 2) * N_Q_HEADS + 2 * lt + par], vp, tn,
                                    preferred_element_type=F32)
                acc = o if acc is None else acc + o
            heads_ref[q0:q0 + WINDOW, lt * LANES:(lt + 1) * LANES] = acc

    mod = mod_ref[0, 0]
    gate = mod[2:3] * pg_ref[0]

    def out_proj_pieces(part):
        rows = slice(part * (TQ_ATT // ATT_SUB), (part + 1) * (TQ_ATT // ATT_SUB))
        n_col = D_MODEL // MXU_DIM

        def piece(c):
            if c == 0:
                attn_ref[rows, :] = _rms(heads_ref[rows, :], ag_ref[0]).astype(BF16)
            cols = slice(c * MXU_DIM, (c + 1) * MXU_DIM)
            heads = jnp.concatenate([attn_ref[rows, :], ssm_ref[rows, :]], axis=1)
            mixed_ref[rows, cols] = jnp.dot(heads, woutb_ref[:, cols], preferred_element_type=F32)
            if c == n_col - 1:
                o_ref[0, rows, :] = x_ref[0, rows, :] + _rms(mixed_ref[rows, :], gate)

        return [functools.partial(piece, c) for c in range(n_col)]

    per_part = nq // ATT_SUB
    queue = []
    for jb in range(nq):
        scores(jb)
        softmax(jb)
        values(jb)
        for _ in range(-(-len(queue) // (per_part - jb % per_part))):
            queue.pop(0)()
        if (jb + 1) % per_part == 0:
            queue.extend(out_proj_pieces(jb // per_part))
    for piece in queue:
        piece()


def _attn(layer, q, kv, ssm_tm, x, mod, sinks, attn_g, w_out, post_g, w1, w2):
    nt = SEQ // TQ_ATT
    per_tile = TQ_ATT // WINDOW
    steps = BATCH * nt
    vec = lambda n: pl.BlockSpec((1, 1, n), lambda b, i: (layer, 0, 0))
    slab = lambda rows, cols: pl.BlockSpec(
        (1, rows // steps, cols), lambda b, i: (layer, b * nt + i, 0))
    slab_out = lambda rows, cols: pl.BlockSpec(
        (1, rows // steps, cols), lambda b, i: (0, b * nt + i, 0))
    return pl.pallas_call(
        _attn_kernel,
        grid=(BATCH, nt),
        in_specs=[
            pl.BlockSpec((1, TQ_ATT, ATTN_WIDTH), lambda b, i: (b, i, 0)),
            pl.BlockSpec((1, TQ_ATT, 4 * KV_WIDTH), lambda b, i: (b, i, 0)),
            pl.BlockSpec((1, WINDOW, 4 * KV_WIDTH),
                         lambda b, i: (b, jnp.maximum(i * per_tile - 1, 0), 0)),
            pl.BlockSpec((TQ_ATT, SSM_WIDTH), lambda b, i: (i, b)),
            pl.BlockSpec((1, TQ_ATT, D_MODEL), lambda b, i: (b, i, 0)),
            pl.BlockSpec((1, 1, N_MOD, D_MODEL), lambda b, i: (layer, b, 0, 0)),
            pl.BlockSpec((1, 1, N_Q_HEADS), lambda b, i: (layer, 0, 0), memory_space=pltpu.SMEM),
            vec(ATTN_WIDTH),
            pl.BlockSpec((1, D_MODEL, D_MODEL), lambda b, i: (layer, 0, 0)),
            vec(D_MODEL),
            slab(D_MODEL, D_FF),
            slab(D_FF, D_MODEL),
        ],
        out_specs=[
            pl.BlockSpec((1, TQ_ATT, D_MODEL), lambda b, i: (b, i, 0)),
            slab_out(D_MODEL, D_FF),
            slab_out(D_FF, D_MODEL),
        ],
        out_shape=[
            jax.ShapeDtypeStruct((BATCH, SEQ, D_MODEL), F32),
            jax.ShapeDtypeStruct((1, D_MODEL, D_FF), BF16),
            jax.ShapeDtypeStruct((1, D_FF, D_MODEL), BF16),
        ],
        scratch_shapes=[
            pltpu.VMEM((2 * N_KV_HEADS, WINDOW + TQ_ATT, LANES), BF16),
            pltpu.VMEM((2 * N_KV_HEADS, WINDOW + TQ_ATT, LANES), BF16),
            pltpu.VMEM((2 * N_Q_HEADS, 2 * WINDOW, WINDOW), F32),
            pltpu.VMEM((TQ_ATT, ATTN_WIDTH), F32),
            pltpu.VMEM((2 * 2 * N_KV_HEADS, 2 * WINDOW, 2 * WINDOW), F32),
            pltpu.VMEM((2 * N_Q_HEADS, 2 * WINDOW, WINDOW), BF16),
            pltpu.VMEM((TQ_ATT, ATTN_WIDTH), BF16),
            pltpu.VMEM((TQ_ATT, D_MODEL), F32),
            pltpu.VMEM((D_MODEL, D_MODEL), BF16),
        ],
        compiler_params=pltpu.CompilerParams(
            dimension_semantics=("arbitrary", "arbitrary"), vmem_limit_bytes=VMEM_LIMIT),
        name="attn_outproj",
    )(q, kv, kv, ssm_tm, x, mod, sinks, attn_g, w_out, post_g, w1, w2)


def _mlp_kernel(x_ref, mod_ref, g_ref, w1_ref, w2_ref, pg_ref, o_ref, acc_ref):
    x = x_ref[0]
    mod = mod_ref[0, 0]
    h = (_rms(x, g_ref[0] * (1.0 + mod[4:5])) + mod[3:4]).astype(BF16)
    for c in range(D_FF // FF_CHUNK):
        cols = slice(c * FF_CHUNK, (c + 1) * FF_CHUNK)
        f = jnp.dot(h, w1_ref[0, :, cols], preferred_element_type=F32)
        f = jnp.square(jnp.maximum(f, 0.0)).astype(BF16)
        part = jnp.dot(f, w2_ref[0, cols, :], preferred_element_type=F32)
        if c == 0:
            acc_ref[...] = part
        else:
            acc_ref[...] += part
    o_ref[0] = x + _rms(acc_ref[...], mod[5:6] * pg_ref[0])


def _mlp(layer, x, mod, pre_g, w1, w2, post_g):
    nt = SEQ // TM_MLP
    vec = lambda n: pl.BlockSpec((1, 1, n), lambda b, i: (layer, 0, 0))
    return pl.pallas_call(
        _mlp_kernel,
        grid=(BATCH, nt),
        in_specs=[
            pl.BlockSpec((1, TM_MLP, D_MODEL), lambda b, i: (b, i, 0)),
            pl.BlockSpec((1, 1, N_MOD, D_MODEL), lambda b, i: (layer, b, 0, 0)),
            vec(D_MODEL),
            pl.BlockSpec((1, D_MODEL, D_FF), lambda b, i: (0, 0, 0), pipeline_mode=pl.Buffered(1)),
            pl.BlockSpec((1, D_FF, D_MODEL), lambda b, i: (0, 0, 0), pipeline_mode=pl.Buffered(1)),
            vec(D_MODEL),
        ],
        out_specs=pl.BlockSpec((1, TM_MLP, D_MODEL), lambda b, i: (b, i, 0)),
        out_shape=jax.ShapeDtypeStruct((BATCH, SEQ, D_MODEL), F32),
        scratch_shapes=[pltpu.VMEM((TM_MLP, D_MODEL), F32)],
        compiler_params=pltpu.CompilerParams(
            dimension_semantics=("arbitrary", "arbitrary"), vmem_limit_bytes=VMEM_LIMIT),
        name="mlp",
    )(x, mod, pre_g, w1, w2, post_g)


def _ssm_params(lam_re, lam_im, log_dt, b_re, b_im, c_re, c_im):
    dt = jnp.exp(log_dt)[..., None]
    mag = jnp.exp(lam_re * dt)
    ang = lam_im * dt
    ab_r = mag * jnp.cos(ang)
    ab_i = mag * jnp.sin(ang)
    nr = ab_r - 1.0
    ni = ab_i
    den = lam_re * lam_re + lam_im * lam_im
    f_r = (nr * lam_re + ni * lam_im) / den
    f_i = (ni * lam_re - nr * lam_im) / den
    bb_r = f_r[..., None] * b_re - f_i[..., None] * b_im
    bb_i = f_r[..., None] * b_im + f_i[..., None] * b_re
    gpt = MXU_DIM // STATE
    gps = LANES // SSM_GROUP
    n_t = STATE_COLS // MXU_DIM
    shape_b = (n_t, LANES, MXU_DIM)
    n_i = lax.broadcasted_iota(jnp.int32, shape_b, 0)
    g_i = lax.broadcasted_iota(jnp.int32, shape_b, 1) // SSM_GROUP
    h_i = lax.broadcasted_iota(jnp.int32, shape_b, 2) // STATE
    mask_b = g_i == gpt * (n_i % (gps // gpt)) + h_i

    def b_tiles(bb):
        blk = bb.reshape(DEPTH, n_t, gpt, STATE, SSM_GROUP).transpose(0, 1, 4, 2, 3)
        blk = blk.reshape(DEPTH, n_t, 1, SSM_GROUP, MXU_DIM)
        rep = jnp.broadcast_to(blk, (DEPTH, n_t, gps, SSM_GROUP, MXU_DIM))
        return jnp.where(mask_b, rep.reshape(DEPTH, n_t, LANES, MXU_DIM), 0.0)

    ba_r = ab_r[..., None] * bb_r - ab_i[..., None] * bb_i
    ba_i = ab_r[..., None] * bb_i + ab_i[..., None] * bb_r
    pair_tiles = lambda first, second: jnp.concatenate([b_tiles(first), b_tiles(second)], axis=2)
    wb = jnp.concatenate([pair_tiles(ba_r, bb_r), pair_tiles(ba_i, bb_i)],
                         axis=1).astype(BF16)

    gpc = MXU_DIM // SSM_GROUP
    shape_c = (SCAN_CHUNKS, SCAN_COLS, MXU_DIM)
    k_i = lax.broadcasted_iota(jnp.int32, shape_c, 0)
    r_i = lax.broadcasted_iota(jnp.int32, shape_c, 1) // STATE
    c_i = lax.broadcasted_iota(jnp.int32, shape_c, 2) // SSM_GROUP
    mask_c = c_i == (gps * k_i + r_i) % gpc

    def c_rows(cc):
        blk = cc.transpose(0, 1, 3, 2).reshape(DEPTH, SCAN_CHUNKS, SCAN_COLS, 1, SSM_GROUP)
        rep = jnp.broadcast_to(blk, (DEPTH, SCAN_CHUNKS, SCAN_COLS, gpc, SSM_GROUP))
        return jnp.where(mask_c, rep.reshape(DEPTH, SCAN_CHUNKS, SCAN_COLS, MXU_DIM), 0.0)

    wc = jnp.concatenate([c_rows(c_re), c_rows(-c_im)], axis=2).astype(BF16)
    ca_r = c_re * ab_r[:, :, None, :] - c_im * ab_i[:, :, None, :]
    ca_i = c_re * ab_i[:, :, None, :] + c_im * ab_r[:, :, None, :]
    wca = jnp.concatenate([c_rows(ca_r), c_rows(-ca_i)], axis=2).astype(BF16)
    hp = lax.Precision.HIGHEST
    feed = (jnp.einsum('lgcp,lgpd->lgdc', c_re, bb_r, precision=hp)
            - jnp.einsum('lgcp,lgpd->lgdc', c_im, bb_i, precision=hp))
    n_out = SSM_WIDTH // MXU_DIM
    blk = feed.reshape(DEPTH, n_out, gpc, SSM_GROUP, 1, SSM_GROUP)
    rep = jnp.broadcast_to(blk, (DEPTH, n_out, gpc, SSM_GROUP, gpc, SSM_GROUP))
    r_g = lax.broadcasted_iota(jnp.int32, (MXU_DIM, MXU_DIM), 0) // SSM_GROUP
    c_g = lax.broadcasted_iota(jnp.int32, (MXU_DIM, MXU_DIM), 1) // SSM_GROUP
    wf = jnp.where(r_g == c_g, rep.reshape(DEPTH, n_out, MXU_DIM, MXU_DIM), 0.0).astype(BF16)
    a2_r = (ab_r * ab_r - ab_i * ab_i).reshape(DEPTH, 1, STATE_COLS)
    a2_i = (2.0 * ab_r * ab_i).reshape(DEPTH, 1, STATE_COLS)
    return wb, a2_r, a2_i, wc, wca, wf


def kernel(x, c, w_ada, b_ada, pre_mix_g, w_in, attn_sinks, lam_re, lam_im, log_dt, b_re, b_im,
           c_re, c_im, d_skip, w_glu, b_glu, attn_out_g, ssm_out_g, w_out, post_mix_g, pre_mlp_g,
           w_mlp_in, w_mlp_out, post_mlp_g):
    row = lambda a: a.reshape(DEPTH, 1, a.shape[-1])
    mod = _modulation(c, w_ada, b_ada).reshape(DEPTH, BATCH, N_MOD, D_MODEL)
    wb, a_re, a_im, wc, wca, wf = _ssm_params(lam_re, lam_im, log_dt, b_re, b_im, c_re, c_im)
    sinks = row(attn_sinks)
    for layer in range(DEPTH):
        q, kv, ssm_tm = _mix(layer, x, mod, row(pre_mix_g), w_in, wb, a_re, a_im, wc, wca, wf,
                             row(d_skip), w_glu, row(b_glu), row(ssm_out_g))
        x, w1_b, w2_b = _attn(layer, q, kv, ssm_tm, x, mod, sinks, row(attn_out_g), w_out,
                              row(post_mix_g), w_mlp_in, w_mlp_out)
        x = _mlp(layer, x, mod, row(pre_mlp_g), w1_b, w2_b, row(post_mlp_g))
    return x
```
